```python
import math
import jax, jax.numpy as jnp
from jax import lax
import numpy as np

D_MODEL = 1024
BATCH = 16
SEQ = 2048
DEPTH = 1
DEC_BATCH = 128
DEC_SEQ = 1
PAST_LEN = 8192
PAGE_SIZE = 128

N_META = 16
A_HEADS = 8
A_HEAD_DIM = 64
A_WIDTH = A_HEADS * A_HEAD_DIM
Q_BLOCK = 128
ATTN_SCALE = 1.0 / math.sqrt(A_HEAD_DIM)
NEG_INF = -1e30
R_HEADS = 8
R_HEAD_DIM = 64
R_WIDTH = R_HEADS * R_HEAD_DIM
DECAY_LORA = 64
AAA_LORA = 64
GATE_LORA = 160
GN_EPS = 64e-5
A_COLS = 3 * A_WIDTH + A_HEADS
R_SHIFT_COLS = 3 * R_WIDTH + DECAY_LORA + AAA_LORA + GATE_LORA
GATE_COLS = 2 * D_MODEL
IN_COLS = A_COLS + R_SHIFT_COLS + GATE_COLS
N_EXPERTS = 32
TOP_K = 4
D_FF = D_MODEL
SWIGLU_ALPHA = 1.702
SWIGLU_LIMIT = 7.0
NORM_EPS = 1e-5

kernel_name = 'fox_rwkv7_gated_hybrid_moe_step'


def rms_norm(x, g):
    xf = x.astype(jnp.float32)
    y = xf * lax.rsqrt(jnp.mean(xf * xf, axis=-1, keepdims=True) + NORM_EPS)
    return (y * g.astype(jnp.float32)).astype(x.dtype)


def fox_attend(q, k, v, c_q, c_k, mask):
    s = jnp.einsum('bqhd,bkhd->bhqk', q, k).astype(jnp.float32) * ATTN_SCALE
    s = s + jnp.transpose(c_q, (0, 2, 1))[..., :, None] - jnp.transpose(c_k, (0, 2, 1))[..., None, :]
    s = jnp.where(mask, s, NEG_INF)
    p = jax.nn.softmax(s, axis=-1).astype(v.dtype)
    return jnp.einsum('bhqk,bkhd->bqhd', p, v)


def fox_prompt(q, k, v, logf):
    b, l, h, dh = q.shape
    n_blk = (l - N_META) // Q_BLOCK
    c = jnp.cumsum(logf, axis=1)
    pos = jnp.arange(l)
    o_meta = fox_attend(q[:, :N_META], k[:, :N_META], v[:, :N_META], c[:, :N_META], c[:, :N_META],
                        pos[:N_META, None] >= pos[None, :N_META])
    q_blk = jnp.moveaxis(q[:, N_META:].reshape(b, n_blk, Q_BLOCK, h, dh), 1, 0)
    c_blk = jnp.moveaxis(c[:, N_META:].reshape(b, n_blk, Q_BLOCK, h), 1, 0)
    p_blk = pos[N_META:].reshape(n_blk, Q_BLOCK)

    def one_block(args):
        q_i, c_i, p_i = args
        return fox_attend(q_i, k, v, c_i, c, p_i[:, None] >= pos[None, :])

    o = lax.map(one_block, (q_blk, c_blk, p_blk))
    o = jnp.moveaxis(o, 0, 1).reshape(b, l - N_META, h, dh)
    return jnp.concatenate([o_meta, o], axis=1)


def fox_decode(q, k, v, logf, k_past, v_past, logf_past):
    t = q.shape[1]
    p_len = k_past.shape[1]
    c = jnp.cumsum(jnp.concatenate([logf_past.astype(jnp.float32), logf], axis=1), axis=1)
    c_past, c_new = c[:, :p_len], c[:, p_len:]
    cq = jnp.transpose(c_new, (0, 2, 1))[..., :, None]
    s_past = jnp.einsum('bqhd,bkhd->bhqk', q, k_past.astype(q.dtype)).astype(jnp.float32) * ATTN_SCALE
    s_past = s_past + cq - jnp.transpose(c_past, (0, 2, 1))[..., None, :]
    s_new = jnp.einsum('bqhd,bkhd->bhqk', q, k).astype(jnp.float32) * ATTN_SCALE
    s_new = s_new + cq - jnp.transpose(c_new, (0, 2, 1))[..., None, :]
    tri = jnp.arange(t)[:, None] >= jnp.arange(t)[None, :]
    s_new = jnp.where(tri, s_new, NEG_INF)
    p = jax.nn.softmax(jnp.concatenate([s_past, s_new], axis=-1), axis=-1).astype(v.dtype)
    return (jnp.einsum('bhqk,bkhd->bqhd', p[..., :p_len], v_past.astype(v.dtype))
            + jnp.einsum('bhqk,bkhd->bqhd', p[..., p_len:], v))


def wkv7_scan(s0, r, w, k, v, kk, a):
    def step(S, inp):
        r_t, w_t, k_t, v_t, kk_t, a_t = inp
        sa = jnp.einsum('bhij,bhj->bhi', S, -kk_t)
        S = S * w_t[:, :, None, :] + sa[..., None] * (kk_t * a_t)[:, :, None, :] + v_t[..., :, None] * k_t[:, :, None, :]
        y = jnp.einsum('bhij,bhj->bhi', S, r_t)
        return S, y
    xs = tuple(jnp.moveaxis(t_, 1, 0) for t_ in (r, w, k, v, kk, a))
    S, y = lax.scan(step, s0, xs)
    return S, jnp.moveaxis(y, 0, 1)


def rwkv7_branch(zr, shift_prev, wkv0, lw):
    b, t, _ = zr.shape
    f32 = jnp.float32
    z_prev = jnp.concatenate([shift_prev.astype(zr.dtype), zr[:, :-1]], axis=1)
    zs = zr + (z_prev - zr) * lw['rwkv_mu']
    o0 = 3 * R_WIDTH
    r = zs[..., :R_WIDTH]
    k = zs[..., R_WIDTH:2 * R_WIDTH]
    v = zs[..., 2 * R_WIDTH:o0]
    zw = zs[..., o0:o0 + DECAY_LORA]
    za = zs[..., o0 + DECAY_LORA:o0 + DECAY_LORA + AAA_LORA]
    zg = zs[..., o0 + DECAY_LORA + AAA_LORA:]
    w_log = -jax.nn.softplus(-(lw['rwkv_w0'] + jnp.tanh(zw) @ lw['rwkv_w2']).astype(f32)) - 0.5
    decay = jnp.exp(-jnp.exp(w_log))
    a = jax.nn.sigmoid((lw['rwkv_a0'] + za @ lw['rwkv_a2']).astype(f32))
    g = jax.nn.sigmoid(zg) @ lw['rwkv_g2']
    hd = lambda u: u.astype(f32).reshape(b, t, R_HEADS, R_HEAD_DIM)
    r_h, k_h, v_h, a_h, w_h = hd(r), hd(k), hd(v), hd(a), hd(decay)
    kk = k_h * lw['rwkv_k_k'].astype(f32).reshape(R_HEADS, R_HEAD_DIM)
    kk = kk / jnp.maximum(jnp.sqrt(jnp.sum(kk * kk, axis=-1, keepdims=True)), 1e-12)
    k_h = k_h * (1.0 + (a_h - 1.0) * lw['rwkv_k_a'].astype(f32).reshape(R_HEADS, R_HEAD_DIM))
    S, y = wkv7_scan(wkv0.astype(f32), r_h, w_h, k_h, v_h, kk, a_h)
    mean = jnp.mean(y, axis=-1, keepdims=True)
    var = jnp.mean(jnp.square(y - mean), axis=-1, keepdims=True)
    yn = ((y - mean) * lax.rsqrt(var + GN_EPS)).reshape(b, t, R_WIDTH)
    yn = yn * lw['rwkv_ln_w'].astype(f32) + lw['rwkv_ln_b'].astype(f32)
    bonus = jnp.sum(r_h * k_h * lw['rwkv_r_k'].astype(f32), axis=-1, keepdims=True) * v_h
    out = (yn + bonus.reshape(b, t, R_WIDTH)).astype(zr.dtype) * g
    return out @ lw['w_or'], S.astype(wkv0.dtype), zr[:, -1:]


def token_mixers(h, attn_fn, shift_prev, wkv0, lw):
    b, t, _ = h.shape
    z = h @ lw['w_in']
    za = z[..., :A_COLS]
    q = za[..., :A_WIDTH].reshape(b, t, A_HEADS, A_HEAD_DIM)
    k = za[..., A_WIDTH:2 * A_WIDTH].reshape(b, t, A_HEADS, A_HEAD_DIM)
    v = za[..., 2 * A_WIDTH:3 * A_WIDTH].reshape(b, t, A_HEADS, A_HEAD_DIM)
    logf = jax.nn.log_sigmoid(za[..., 3 * A_WIDTH:].astype(jnp.float32) + lw['b_f'].astype(jnp.float32))
    o_attn = attn_fn(q, k, v, logf).reshape(b, t, A_WIDTH) @ lw['w_oa']
    zr = z[..., A_COLS:A_COLS + R_SHIFT_COLS]
    o_rwkv, wkv_new, shift_new = rwkv7_branch(zr, shift_prev, wkv0, lw)
    zg = z[..., A_COLS + R_SHIFT_COLS:]
    merged = jax.nn.sigmoid(zg[..., :D_MODEL]) * o_attn + jax.nn.sigmoid(zg[..., D_MODEL:]) * o_rwkv
    return merged @ lw['w_out'], (k, v, logf, wkv_new, shift_new)


def moe(h, lw):
    shp = h.shape
    tok = h.reshape(-1, shp[-1])
    logits = (tok @ lw['w_router'] + lw['b_router']).astype(jnp.float32)
    top_v, top_i = lax.top_k(logits, TOP_K)
    gate_w = jax.nn.softmax(top_v, axis=-1)
    dense_gate = jnp.sum(jax.nn.one_hot(top_i, N_EXPERTS, dtype=jnp.float32) * gate_w[..., None], axis=1).astype(tok.dtype)
    out = jnp.zeros_like(tok)
    for e in range(N_EXPERTS):
        u = tok @ lw['w_up'][e] + lw['b_up'][e]
        x_glu = jnp.minimum(u[:, :D_FF], SWIGLU_LIMIT)
        x_lin = jnp.clip(u[:, D_FF:], -SWIGLU_LIMIT, SWIGLU_LIMIT)
        act = x_glu * jax.nn.sigmoid(SWIGLU_ALPHA * x_glu) * (x_lin + 1.0)
        out = out + dense_gate[:, e:e + 1] * (act @ lw['w_down'][e] + lw['b_down'][e])
    return out.reshape(shp)


def trunk_layer(x, attn_fn, shift_prev, wkv0, lw):
    mix, state = token_mixers(rms_norm(x, lw['norm_mix']), attn_fn, shift_prev, wkv0, lw)
    x = x + mix
    x = x + moe(rms_norm(x, lw['norm_ffn']), lw)
    return x, state


def setup_inputs(seed: int = 0) -> dict:
    key = jax.random.key(seed)
    ks = iter(jax.random.split(key, 48))
    f32 = jnp.float32

    def nrm(shape, scale):
        return jax.random.normal(next(ks), shape, f32) * scale

    n_pages = PAST_LEN // PAGE_SIZE
    n_used = DEC_BATCH * n_pages
    n_pool = (5 * n_used + 3) // 4
    perm = jax.random.permutation(next(ks), n_pool)[:n_used]
    page_table = perm.reshape(DEC_BATCH, n_pages).astype(jnp.int32)
    head_ramp = (jnp.zeros((R_HEADS, 1), f32) + jnp.linspace(-6.0, -1.0, R_HEAD_DIM)[None, :]).reshape(-1)
    return {
        'x_prompt': nrm((BATCH, SEQ, D_MODEL), 1.0),
        'x_sample': nrm((DEC_BATCH, DEC_SEQ, D_MODEL), 1.0),
        'cache_k': nrm((DEPTH, n_pool, PAGE_SIZE, A_HEADS, A_HEAD_DIM), 1.0),
        'cache_v': nrm((DEPTH, n_pool, PAGE_SIZE, A_HEADS, A_HEAD_DIM), 1.0),
        'cache_logf': jax.nn.log_sigmoid(4.5 + nrm((DEPTH, n_pool, PAGE_SIZE, A_HEADS), 1.0)),
        'page_table': page_table,
        'state_wkv': nrm((DEPTH, DEC_BATCH, R_HEADS, R_HEAD_DIM, R_HEAD_DIM), 0.3),
        'state_shift': nrm((DEPTH, DEC_BATCH, 1, R_SHIFT_COLS), 1.0),
        'meta_tokens': nrm((N_META, D_MODEL), 1.0),
        'norm_mix': 1.0 + nrm((DEPTH, D_MODEL), 0.02),
        'w_in': nrm((DEPTH, D_MODEL, IN_COLS), D_MODEL ** -0.5),
        'b_f': jnp.linspace(2.0, 7.0, A_HEADS)[None, :] + nrm((DEPTH, A_HEADS), 0.1),
        'w_oa': nrm((DEPTH, A_WIDTH, D_MODEL), A_WIDTH ** -0.5),
        'rwkv_mu': jax.random.uniform(next(ks), (DEPTH, R_SHIFT_COLS), f32),
        'rwkv_w0': head_ramp[None, :] + nrm((DEPTH, R_WIDTH), 0.1),
        'rwkv_w2': nrm((DEPTH, DECAY_LORA, R_WIDTH), DECAY_LORA ** -0.5),
        'rwkv_a0': nrm((DEPTH, R_WIDTH), 0.1),
        'rwkv_a2': nrm((DEPTH, AAA_LORA, R_WIDTH), AAA_LORA ** -0.5),
        'rwkv_g2': nrm((DEPTH, GATE_LORA, R_WIDTH), GATE_LORA ** -0.5),
        'rwkv_k_k': 0.85 + nrm((DEPTH, R_WIDTH), 0.02),
        'rwkv_k_a': 1.0 + nrm((DEPTH, R_WIDTH), 0.02),
        'rwkv_r_k': nrm((DEPTH, R_HEADS, R_HEAD_DIM), 0.1),
        'rwkv_ln_w': 1.0 + nrm((DEPTH, R_WIDTH), 0.02),
        'rwkv_ln_b': nrm((DEPTH, R_WIDTH), 0.02),
        'w_or': nrm((DEPTH, R_WIDTH, D_MODEL), R_WIDTH ** -0.5),
        'w_out': nrm((DEPTH, D_MODEL, D_MODEL), D_MODEL ** -0.5),
        'norm_ffn': 1.0 + nrm((DEPTH, D_MODEL), 0.02),
        'w_router': nrm((DEPTH, D_MODEL, N_EXPERTS), D_MODEL ** -0.5),
        'b_router': nrm((DEPTH, N_EXPERTS), 0.01),
        'w_up': nrm((DEPTH, N_EXPERTS, D_MODEL, 2 * D_FF), D_MODEL ** -0.5),
        'b_up': nrm((DEPTH, N_EXPERTS, 2 * D_FF), 0.01),
        'w_down': nrm((DEPTH, N_EXPERTS, D_FF, D_MODEL), D_FF ** -0.5),
        'b_down': nrm((DEPTH, N_EXPERTS, D_MODEL), 0.01),
        'norm_final': 1.0 + nrm((D_MODEL,), 0.02),
    }


def reference(x_prompt, x_sample, cache_k, cache_v, cache_logf, page_table, state_wkv, state_shift,
              meta_tokens, norm_mix, w_in, b_f, w_oa, rwkv_mu, rwkv_w0, rwkv_w2, rwkv_a0, rwkv_a2,
              rwkv_g2, rwkv_k_k, rwkv_k_a, rwkv_r_k, rwkv_ln_w, rwkv_ln_b, w_or, w_out, norm_ffn,
              w_router, b_router, w_up, b_up, w_down, b_down, norm_final):
    b = x_prompt.shape[0]
    db = x_sample.shape[0]
    past_len = page_table.shape[1] * cache_k.shape[2]
    meta = jnp.broadcast_to(meta_tokens.astype(x_prompt.dtype)[None], (b, N_META, x_prompt.shape[-1]))
    xp = jnp.concatenate([meta, x_prompt], axis=1)
    xs = x_sample
    zero_shift = jnp.zeros((b, 1, R_SHIFT_COLS), x_prompt.dtype)
    zero_wkv = jnp.zeros((b, R_HEADS, R_HEAD_DIM, R_HEAD_DIM), state_wkv.dtype)
    kp, vp, lp, wp, sp = [], [], [], [], []
    ksl, vsl, lsl, wsl, ssl = [], [], [], [], []
    for l in range(DEPTH):
        lw = {
            'norm_mix': norm_mix[l], 'w_in': w_in[l], 'b_f': b_f[l], 'w_oa': w_oa[l],
            'rwkv_mu': rwkv_mu[l], 'rwkv_w0': rwkv_w0[l], 'rwkv_w2': rwkv_w2[l], 'rwkv_a0': rwkv_a0[l],
            'rwkv_a2': rwkv_a2[l], 'rwkv_g2': rwkv_g2[l], 'rwkv_k_k': rwkv_k_k[l], 'rwkv_k_a': rwkv_k_a[l],
            'rwkv_r_k': rwkv_r_k[l], 'rwkv_ln_w': rwkv_ln_w[l], 'rwkv_ln_b': rwkv_ln_b[l], 'w_or': w_or[l],
            'w_out': w_out[l], 'norm_ffn': norm_ffn[l], 'w_router': w_router[l], 'b_router': b_router[l],
            'w_up': w_up[l], 'b_up': b_up[l], 'w_down': w_down[l], 'b_down': b_down[l],
        }
        k_past = cache_k[l, page_table].reshape(db, past_len, A_HEADS, A_HEAD_DIM)
        v_past = cache_v[l, page_table].reshape(db, past_len, A_HEADS, A_HEAD_DIM)
        lf_past = cache_logf[l, page_table].reshape(db, past_len, A_HEADS)

        def attn_sample(q, k, v, lf, k_past=k_past, v_past=v_past, lf_past=lf_past):
            return fox_decode(q, k, v, lf, k_past, v_past, lf_past)

        xp, st_p = trunk_layer(xp, fox_prompt, zero_shift, zero_wkv, lw)
        xs, st_s = trunk_layer(xs, attn_sample, state_shift[l], state_wkv[l], lw)
        kp.append(st_p[0]); vp.append(st_p[1]); lp.append(st_p[2]); wp.append(st_p[3]); sp.append(st_p[4])
        ksl.append(st_s[0]); vsl.append(st_s[1]); lsl.append(st_s[2]); wsl.append(st_s[3]); ssl.append(st_s[4])
    y_prompt = rms_norm(xp, norm_final)[:, N_META:]
    y_sample = rms_norm(xs, norm_final)
    k_prompt, v_prompt, logf_prompt = jnp.stack(kp), jnp.stack(vp), jnp.stack(lp)
    wkv_prompt, shift_prompt = jnp.stack(wp), jnp.stack(sp)
    k_sample, v_sample, logf_sample = jnp.stack(ksl), jnp.stack(vsl), jnp.stack(lsl)
    wkv_sample, shift_sample = jnp.stack(wsl), jnp.stack(ssl)
    return (y_prompt, y_sample, k_prompt, v_prompt, logf_prompt, k_sample, v_sample, logf_sample,
            wkv_prompt, shift_prompt, wkv_sample, shift_sample)
```

```python
import functools
import math

import jax
import jax.numpy as jnp
from jax import lax
from jax.experimental import pallas as pl
from jax.experimental.pallas import tpu as pltpu

F32 = jnp.float32
BF16 = jnp.bfloat16
HIGHEST = lax.Precision.HIGHEST

N_META = 16
HEADS = 8
HEAD_DIM = 64
WIDTH = HEADS * HEAD_DIM
DECAY_LORA = 64
AAA_LORA = 64
GATE_LORA = 160
ATTN_SCALE = 1.0 / math.sqrt(HEAD_DIM)
NEG_INF = -1e30
GN_EPS = 64e-5
NORM_EPS = 1e-5
TOP_K = 4
SWIGLU_ALPHA = 1.702
SWIGLU_LIMIT = 7.0

LANES = 128
QKV_COLS = 3 * WIDTH
F_OFF = QKV_COLS
F_COLS = LANES
R_OFF = F_OFF + F_COLS
ZW_OFF = 3 * WIDTH
ZA_OFF = ZW_OFF + LANES
ZG_OFF = ZA_OFF + LANES
ZG_COLS = 2 * LANES
R_COLS = ZG_OFF + ZG_COLS
G_OFF = R_OFF + R_COLS
VMEM_LIMIT = 56 * 1024 * 1024


def _cparams(sem):
    return pltpu.CompilerParams(dimension_semantics=sem, vmem_limit_bytes=VMEM_LIMIT)


def _full(shape):
    n = len(shape)
    return pl.BlockSpec(shape, lambda *_: (0,) * n)


def _log_sigmoid(x):
    return jnp.minimum(x, 0.0) - jnp.log1p(jnp.exp(-jnp.abs(x)))


def _sigmoid(x):
    return 1.0 / (1.0 + jnp.exp(-x))


def _rms(x, g):
    return x * lax.rsqrt(jnp.mean(x * x, axis=-1, keepdims=True) + NORM_EPS) * g


def _dot(a, b, prec=None):
    return jnp.dot(a, b, preferred_element_type=F32, precision=prec)


def _dot_nt(a, b, prec=None):
    return lax.dot_general(a, b, (((1,), (1,)), ((), ())), preferred_element_type=F32, precision=prec)


def _dot_tn(a, b, prec=None):
    return lax.dot_general(a, b, (((0,), (0,)), ((), ())), preferred_element_type=F32, precision=prec)


def _inproj_kernel(x_ref, g_ref, w_ref, bf_ref, qb_ref, kb_ref, vb_ref, k_ref, v_ref, lf_ref, zr_ref, sg_ref):
    h = _rms(x_ref[...], g_ref[...]).astype(BF16)

    def mm(lo, n):
        return _dot(h, w_ref[:, lo:lo + n])

    qb_ref[...] = (mm(0, WIDTH) * ATTN_SCALE).astype(BF16)
    k = mm(WIDTH, WIDTH)
    k_ref[...] = k
    kb_ref[...] = k.astype(BF16)
    v = mm(2 * WIDTH, WIDTH)
    v_ref[...] = v
    vb_ref[...] = v.astype(BF16)
    lf_ref[...] = _log_sigmoid(mm(F_OFF, F_COLS) + bf_ref[...])
    for c in range(R_COLS // WIDTH):
        zr_ref[:, c * WIDTH:(c + 1) * WIDTH] = mm(R_OFF + c * WIDTH, WIDTH)
    for c in range(2 * 1024 // WIDTH):
        sg_ref[:, c * WIDTH:(c + 1) * WIDTH] = _sigmoid(mm(G_OFF + c * WIDTH, WIDTH)).astype(BF16)


def _inproj(x, g, w_packed, bf_pad, tm):
    m, d = x.shape
    ncols = w_packed.shape[1]
    row = lambda n: pl.BlockSpec((tm, n), lambda i: (i, 0))
    outs = [
        jax.ShapeDtypeStruct((m, WIDTH), BF16), jax.ShapeDtypeStruct((m, WIDTH), BF16),
        jax.ShapeDtypeStruct((m, WIDTH), BF16), jax.ShapeDtypeStruct((m, WIDTH), F32),
        jax.ShapeDtypeStruct((m, WIDTH), F32), jax.ShapeDtypeStruct((m, F_COLS), F32),
        jax.ShapeDtypeStruct((m, R_COLS), F32), jax.ShapeDtypeStruct((m, 2 * d), BF16),
    ]
    return pl.pallas_call(
        _inproj_kernel,
        out_shape=outs,
        grid=(pl.cdiv(m, tm),),
        in_specs=[row(d), _full((1, d)), _full((d, ncols)), _full((1, F_COLS))],
        out_specs=[row(WIDTH), row(WIDTH), row(WIDTH), row(WIDTH), row(WIDTH), row(F_COLS), row(R_COLS),
                   row(2 * d)],
        compiler_params=_cparams(("parallel",)),
        name="inproj",
    )(x, g, w_packed, bf_pad)


CUM_BLOCK = 256


def _cumsum_kernel(lf_ref, c_ref):
    length = lf_ref.shape[1]
    r = lax.broadcasted_iota(jnp.int32, (CUM_BLOCK, CUM_BLOCK), 0)
    c = lax.broadcasted_iota(jnp.int32, (CUM_BLOCK, CUM_BLOCK), 1)
    tri = (c <= r).astype(F32)
    carry = jnp.zeros((1, lf_ref.shape[2]), F32)
    for lo in range(0, length, CUM_BLOCK):
        n = min(CUM_BLOCK, length - lo)
        blk = _dot(tri[:n, :n], lf_ref[0, lo:lo + n, :], HIGHEST) + carry
        c_ref[0, lo:lo + n, :] = blk
        carry = blk[n - 1:n, :]


def _cumsum_logf(lf):
    b, length, n = lf.shape
    spec = pl.BlockSpec((1, length, n), lambda i: (i, 0, 0))
    return pl.pallas_call(
        _cumsum_kernel,
        out_shape=jax.ShapeDtypeStruct(lf.shape, F32),
        grid=(b,),
        in_specs=[spec],
        out_specs=spec,
        compiler_params=_cparams(("parallel",)),
        name="cumsum_logf",
    )(lf)


QB = 128
PAIR = 2


def _attn_kernel(q_ref, k_ref, v_ref, c_ref, ct_ref, o_ref, *, n_full, tail):
    grp = pl.program_id(1)
    lane = lax.broadcasted_iota(jnp.int32, (1, LANES), 1)
    lane_head = lane // HEAD_DIM

    def attend(qt, ctile, q_rows, n_off, diag_k, diag_v, diag_ck_lo):
        outs = []
        for hh in range(PAIR):
            head = grp * PAIR + hh
            qm = jnp.where(lane_head == hh, qt, jnp.zeros_like(qt))
            cq = jnp.sum(jnp.where(lane == head, ctile, 0.0), axis=-1, keepdims=True)

            def step(s, vt, carry):
                m, l, acc = carry
                m_new = jnp.maximum(m, jnp.max(s, axis=-1, keepdims=True))
                alpha = jnp.exp(m - m_new)
                p = jnp.exp(s - m_new)
                l = alpha * l + jnp.sum(p, axis=-1, keepdims=True)
                acc = alpha * acc + _dot(p.astype(BF16), vt)
                return m_new, l, acc

            def body(j, carry):
                c0 = pl.multiple_of(j * QB, QB)
                kt = k_ref[0, pl.ds(c0, QB), :]
                vt = v_ref[0, pl.ds(c0, QB), :]
                ck = ct_ref[0, head, :, pl.ds(c0, QB)]
                return step(_dot_nt(qm, kt) + cq - ck, vt, carry)

            init = (jnp.full((q_rows, 1), NEG_INF, F32), jnp.zeros((q_rows, 1), F32),
                    jnp.zeros((q_rows, LANES), F32))
            carry = lax.fori_loop(0, n_off, body, init)
            ck = ct_ref[0, head, :, pl.ds(diag_ck_lo, QB)]
            s = _dot_nt(qm, diag_k) + cq - ck
            ri = lax.broadcasted_iota(jnp.int32, (q_rows, QB), 0)
            ci = lax.broadcasted_iota(jnp.int32, (q_rows, QB), 1)
            s = jnp.where(ci <= ri, s, NEG_INF)
            _, l, acc = step(s, diag_v, carry)
            outs.append(acc / l)
        return jnp.where(lane_head == 0, outs[0], outs[1]).astype(o_ref.dtype)

    def qblock(i, _):
        r0 = pl.multiple_of(i * QB, QB)
        qt = q_ref[0, pl.ds(r0, QB), :]
        ctile = c_ref[0, pl.ds(r0, QB), :]
        kd = k_ref[0, pl.ds(r0, QB), :]
        vd = v_ref[0, pl.ds(r0, QB), :]
        o_ref[0, pl.ds(r0, QB), :] = attend(qt, ctile, QB, i, kd, vd, r0)
        return 0

    lax.fori_loop(0, n_full, qblock, 0)
    if tail:
        lo = n_full * QB
        pad = jnp.zeros((QB - tail, LANES), k_ref.dtype)
        kd = jnp.concatenate([k_ref[0, lo:lo + tail, :], pad], axis=0)
        vd = jnp.concatenate([v_ref[0, lo:lo + tail, :], pad], axis=0)
        o_ref[0, lo:lo + tail, :] = attend(q_ref[0, lo:lo + tail, :], c_ref[0, lo:lo + tail, :], tail, n_full,
                                           kd, vd, lo)


def _fox_prompt(qb, kb, vb, c_tok, c_t):
    b, length, _ = qb.shape
    n_full, tail = divmod(length, QB)
    assert tail % 16 == 0 and c_t.shape[3] == (n_full + (1 if tail else 0)) * QB
    qkv = pl.BlockSpec((1, length, LANES), lambda i, g: (i, 0, g))
    return pl.pallas_call(
        functools.partial(_attn_kernel, n_full=n_full, tail=tail),
        out_shape=jax.ShapeDtypeStruct(qb.shape, BF16),
        grid=(b, HEADS // PAIR),
        in_specs=[qkv, qkv, qkv,
                  pl.BlockSpec((1, length, LANES), lambda i, g: (i, 0, 0)),
                  pl.BlockSpec((1, HEADS, 1, c_t.shape[3]), lambda i, g: (i, 0, 0, 0))],
        out_specs=qkv,
        compiler_params=_cparams(("parallel", "arbitrary")),
        name="fox_prompt",
    )(qb, kb, vb, c_tok, c_t)


def _head_sum_matrix():
    r = lax.broadcasted_iota(jnp.int32, (WIDTH, WIDTH), 0) // HEAD_DIM
    c = lax.broadcasted_iota(jnp.int32, (WIDTH, WIDTH), 1) // HEAD_DIM
    return (r == c).astype(F32)


def _rwkv_pointwise(zs, p):
    (w0, w2, a0, a2, g2, k_k, k_a, r_k) = p
    r = zs[:, 0:WIDTH]
    k = zs[:, WIDTH:2 * WIDTH]
    v = zs[:, 2 * WIDTH:3 * WIDTH]
    zw = zs[:, ZW_OFF:ZW_OFF + LANES]
    za = zs[:, ZA_OFF:ZA_OFF + LANES]
    zg = zs[:, ZG_OFF:ZG_OFF + ZG_COLS]
    w_log = _log_sigmoid(w0 + _dot(jnp.tanh(zw).astype(BF16), w2)) - 0.5
    logdec = -jnp.exp(w_log)
    a = _sigmoid(a0 + _dot(za.astype(BF16), a2))
    g = _dot(_sigmoid(zg).astype(BF16), g2)
    hs = _head_sum_matrix()
    kk = k * k_k
    norm = jnp.sqrt(_dot(kk * kk, hs, HIGHEST))
    kk = kk / jnp.maximum(norm, 1e-12)
    km = k * (1.0 + (a - 1.0) * k_a)
    bonus = _dot(r * km * r_k, hs, HIGHEST) * v
    return r, km, v, kk, a, logdec, g, bonus


def _group_norm_out(y, ln_w, ln_b, bonus, g):
    mean = jnp.mean(y, axis=-1, keepdims=True)
    var = jnp.mean(jnp.square(y - mean), axis=-1, keepdims=True)
    yn = (y - mean) * lax.rsqrt(var + GN_EPS) * ln_w + ln_b
    return (yn + bonus) * g


RCHUNK = 64
RPREC = HIGHEST


def _rwkv_chunk_kernel(zr_ref, zp_ref, sh_ref, mu_ref, w0_ref, w2_ref, a0_ref, a2_ref, g2_ref, kk_ref, ka_ref,
                       rk_ref, lnw_ref, lnb_ref, out_ref, wkv_ref, s_ref, *, length):
    ci = pl.program_id(1)
    n = RCHUNK

    @pl.when(ci == 0)
    def _():
        s_ref[...] = jnp.zeros_like(s_ref)

    zr = zr_ref[0]
    row = lax.broadcasted_iota(jnp.int32, (n, 1), 0)
    prev = jnp.where(ci == 0, sh_ref[0], zp_ref[0, 7:8, :])
    zprev = jnp.where(row == 0, prev, pltpu.roll(zr, 1, axis=0))
    valid = (ci * n + row) < length
    zs = jnp.where(valid, zr + (zprev - zr) * mu_ref[...], 0.0)
    params = (w0_ref[...], w2_ref[...], a0_ref[...], a2_ref[...], g2_ref[...], kk_ref[...], ka_ref[...],
              rk_ref[...])
    r, km, v, kk, a, logdec, g, bonus = _rwkv_pointwise(zs, params)
    logdec = jnp.where(valid, logdec, 0.0)

    ri = lax.broadcasted_iota(jnp.int32, (n, n), 0)
    cj = lax.broadcasted_iota(jnp.int32, (n, n), 1)
    incl = cj <= ri
    strict = cj < ri
    eye = (ri == cj).astype(F32)
    logp = _dot(incl.astype(F32), logdec, HIGHEST)
    logp_c = logp[n - 1:n, :]
    e_pos = jnp.exp(logp)
    e_neg = jnp.exp(-logp)
    e_end = jnp.exp(logp_c - logp)
    rt_all = r * e_pos
    kt_all = km * e_neg
    at_all = -kk * jnp.exp(logp - logdec)
    bt_all = kk * a * e_neg
    bh_all = kk * a * e_end
    kh_all = km * e_end
    pc_all = jnp.exp(logp_c)

    for h in range(HEADS):
        sl = slice(h * HEAD_DIM, (h + 1) * HEAD_DIM)
        rt, kt, at, bt, bh, kh, vh = (x[:, sl] for x in (rt_all, kt_all, at_all, bt_all, bh_all, kh_all, v))
        lhs = jnp.concatenate([at, rt], axis=0)
        m_b = _dot_nt(lhs, bt, RPREC)
        m_k = _dot_nt(lhs, kt, RPREC)
        a_ab = jnp.where(strict, m_b[:n], 0.0)
        a_ak = jnp.where(strict, m_k[:n], 0.0)
        m_rb = jnp.where(incl, m_b[n:], 0.0)
        m_rk = jnp.where(incl, m_k[n:], 0.0)
        t_inv = eye + a_ab
        pw = _dot(a_ab, a_ab, RPREC)
        span = 2
        while span < n:
            t_inv = t_inv + _dot(pw, t_inv, RPREC)
            span *= 2
            if span < n:
                pw = _dot(pw, pw, RPREC)
        w_m = _dot(t_inv, at, RPREC)
        u0 = _dot(t_inv, _dot(a_ak, vh, RPREC), RPREC)
        qp = rt + _dot(m_rb, w_m, RPREC)
        y0 = _dot(m_rb, u0, RPREC) + _dot(m_rk, vh, RPREC)
        gt = eye * pc_all[:, sl] + _dot_tn(w_m, bh, RPREC)
        et = _dot_tn(u0, bh, RPREC) + _dot_tn(vh, kh, RPREC)
        s_h = s_ref[h]
        y = _dot_nt(qp, s_h, RPREC) + y0
        s_ref[h] = _dot(s_h, gt, RPREC) + et
        out_ref[0, :, sl] = _group_norm_out(y, lnw_ref[:, sl], lnb_ref[:, sl], bonus[:, sl],
                                            g[:, sl]).astype(out_ref.dtype)

    @pl.when(ci == pl.num_programs(1) - 1)
    def _():
        wkv_ref[0] = s_ref[...]


def _rwkv_prompt(zr, shift0, rp):
    b, length, _ = zr.shape
    n = RCHUNK
    nchunk = pl.cdiv(length, n)
    vec = _full((1, WIDTH))
    return pl.pallas_call(
        functools.partial(_rwkv_chunk_kernel, length=length),
        out_shape=[jax.ShapeDtypeStruct((b, length, WIDTH), BF16),
                   jax.ShapeDtypeStruct((b, HEADS, HEAD_DIM, HEAD_DIM), F32)],
        grid=(b, nchunk),
        in_specs=[pl.BlockSpec((1, n, R_COLS), lambda i, c: (i, c, 0)),
                  pl.BlockSpec((1, 8, R_COLS), lambda i, c: (i, jnp.maximum(c * (n // 8) - 1, 0), 0)),
                  pl.BlockSpec((1, 1, R_COLS), lambda i, c: (i, 0, 0)),
                  _full((1, R_COLS)), vec, _full((LANES, WIDTH)), vec, _full((LANES, WIDTH)),
                  _full((ZG_COLS, WIDTH)), vec, vec, vec, vec, vec],
        out_specs=[pl.BlockSpec((1, n, WIDTH), lambda i, c: (i, c, 0)),
                   pl.BlockSpec((1, HEADS, HEAD_DIM, HEAD_DIM), lambda i, c: (i, 0, 0, 0))],
        scratch_shapes=[pltpu.VMEM((HEADS, HEAD_DIM, HEAD_DIM), F32)],
        compiler_params=_cparams(("parallel", "arbitrary")),
        name="rwkv_prompt",
    )(zr, zr, shift0, rp["mu"], rp["w0"], rp["w2"], rp["a0"], rp["a2"], rp["g2"], rp["k_k"], rp["k_a"],
      rp["r_k"], rp["ln_w"], rp["ln_b"])


DEC_ROWS = 8


def _rwkv_decode_kernel(zr_ref, sh_ref, st_ref, mu_ref, w0_ref, w2_ref, a0_ref, a2_ref, g2_ref, kk_ref, ka_ref,
                        rk_ref, lnw_ref, lnb_ref, out_ref, ns_ref, vec_ref):
    zr = zr_ref[...]
    zs = zr + (sh_ref[...] - zr) * mu_ref[...]
    params = (w0_ref[...], w2_ref[...], a0_ref[...], a2_ref[...], g2_ref[...], kk_ref[...], ka_ref[...],
              rk_ref[...])
    r, km, v, kk, a, logdec, g, bonus = _rwkv_pointwise(zs, params)
    for idx, x in enumerate((r, km, v, kk, kk * a, jnp.exp(logdec), g, bonus)):
        for s in range(DEC_ROWS):
            vec_ref[idx, s] = x[s:s + 1, :]
    ri = lax.broadcasted_iota(jnp.int32, (HEAD_DIM, HEAD_DIM), 0)
    cj = lax.broadcasted_iota(jnp.int32, (HEAD_DIM, HEAD_DIM), 1)
    eye = ri == cj

    def per_row(s, _):
        for h in range(HEADS):
            sl = slice(h * HEAD_DIM, (h + 1) * HEAD_DIM)
            r_h, k_h, v_h, kk_h, ka_h, w_h, g_h, bo_h = (vec_ref[i, s, :, sl] for i in range(8))
            st = st_ref[s, h]
            sa = jnp.sum(st * (-kk_h), axis=-1, keepdims=True)
            v_col = jnp.sum(jnp.where(eye, v_h, 0.0), axis=-1, keepdims=True)
            st = st * w_h + sa * ka_h + v_col * k_h
            ns_ref[s, h] = st
            y_col = jnp.sum(st * r_h, axis=-1, keepdims=True)
            y = jnp.sum(jnp.where(eye, y_col, 0.0), axis=0, keepdims=True)
            out_ref[s, :, sl] = _group_norm_out(y, lnw_ref[:, sl], lnb_ref[:, sl], bo_h, g_h)
        return 0

    lax.fori_loop(0, DEC_ROWS, per_row, 0)


def _rwkv_decode(zr, shift, state, rp):
    n = zr.shape[0]
    vec = _full((1, WIDTH))
    rows = lambda c: pl.BlockSpec((DEC_ROWS, c), lambda i: (i, 0))
    st = pl.BlockSpec((DEC_ROWS, HEADS, HEAD_DIM, HEAD_DIM), lambda i: (i, 0, 0, 0))
    return pl.pallas_call(
        _rwkv_decode_kernel,
        out_shape=[jax.ShapeDtypeStruct((n, 1, WIDTH), F32), jax.ShapeDtypeStruct(state.shape, F32)],
        grid=(n // DEC_ROWS,),
        in_specs=[rows(R_COLS), rows(R_COLS), st, _full((1, R_COLS)), vec, _full((LANES, WIDTH)), vec,
                  _full((LANES, WIDTH)), _full((ZG_COLS, WIDTH)), vec, vec, vec, vec, vec],
        out_specs=[pl.BlockSpec((DEC_ROWS, 1, WIDTH), lambda i: (i, 0, 0)), st],
        scratch_shapes=[pltpu.VMEM((8, DEC_ROWS, 1, WIDTH), F32)],
        compiler_params=_cparams(("parallel",)),
        name="rwkv_decode",
    )(zr, shift, state, rp["mu"], rp["w0"], rp["w2"], rp["a0"], rp["a2"], rp["g2"], rp["k_k"], rp["k_a"],
      rp["r_k"], rp["ln_w"], rp["ln_b"])


def _fox_decode_kernel(pt_ref, q_ref, kn_ref, vn_ref, lfn_ref, ck_ref, cv_ref, clf_ref, o_ref, m_ref, l_ref,
                       acc_ref, bias_ref):
    del pt_ref
    p = pl.program_id(1)
    page = ck_ref.shape[1]
    sub = lax.broadcasted_iota(jnp.int32, (HEADS, WIDTH), 0)
    lane_head = lax.broadcasted_iota(jnp.int32, (HEADS, WIDTH), 1) // HEAD_DIM
    hmask = sub == lane_head
    qm = jnp.where(hmask, q_ref[0], 0.0)

    @pl.when(p == 0)
    def _():
        m_ref[...] = jnp.full_like(m_ref, NEG_INF)
        l_ref[...] = jnp.zeros_like(l_ref)
        acc_ref[...] = jnp.zeros_like(acc_ref)
        bias_ref[...] = lfn_ref[0]

    lf = clf_ref[0]
    u = lax.broadcasted_iota(jnp.int32, (page, page), 0)
    t = lax.broadcasted_iota(jnp.int32, (page, page), 1)
    after = _dot(lf, (u > t).astype(F32), HIGHEST)
    s = _dot_nt(qm.astype(BF16), ck_ref[0].astype(BF16)) + bias_ref[...] + after
    m_new = jnp.maximum(m_ref[...], jnp.max(s, axis=-1, keepdims=True))
    alpha = jnp.exp(m_ref[...] - m_new)
    pr = jnp.exp(s - m_new)
    l_ref[...] = alpha * l_ref[...] + jnp.sum(pr, axis=-1, keepdims=True)
    acc_ref[...] = alpha * acc_ref[...] + _dot(pr.astype(BF16), cv_ref[0].astype(BF16))
    m_ref[...] = m_new
    bias_ref[...] = bias_ref[...] + jnp.sum(lf, axis=-1, keepdims=True)

    @pl.when(p == pl.num_programs(1) - 1)
    def _():
        s_new = jnp.sum(qm * kn_ref[0], axis=-1, keepdims=True)
        m_fin = jnp.maximum(m_ref[...], s_new)
        a_fin = jnp.exp(m_ref[...] - m_fin)
        p_new = jnp.exp(s_new - m_fin)
        l_fin = a_fin * l_ref[...] + p_new
        acc = (a_fin * acc_ref[...] + p_new * vn_ref[0]) / l_fin
        o_ref[0] = jnp.sum(jnp.where(hmask, acc, 0.0), axis=0, keepdims=True).astype(o_ref.dtype)


def _fox_decode(page_table, q, k_new, v_new, lf_new, cache_k, cache_v, cache_lft):
    n, n_pages = page_table.shape
    page = cache_k.shape[1]
    tok = pl.BlockSpec((1, 1, WIDTH), lambda s, p, pt: (s, 0, 0))
    paged = lambda shape: pl.BlockSpec(shape, lambda s, p, pt: (pt[s, n_pages - 1 - p], 0, 0))
    grid_spec = pltpu.PrefetchScalarGridSpec(
        num_scalar_prefetch=1,
        grid=(n, n_pages),
        in_specs=[tok, tok, tok, pl.BlockSpec((1, HEADS, 1), lambda s, p, pt: (s, 0, 0)),
                  paged((1, page, WIDTH)), paged((1, page, WIDTH)), paged((1, HEADS, page))],
        out_specs=tok,
        scratch_shapes=[pltpu.VMEM((HEADS, 1), F32), pltpu.VMEM((HEADS, 1), F32), pltpu.VMEM((HEADS, WIDTH), F32),
                        pltpu.VMEM((HEADS, 1), F32)],
    )
    return pl.pallas_call(
        _fox_decode_kernel,
        out_shape=jax.ShapeDtypeStruct((n, 1, WIDTH), BF16),
        grid_spec=grid_spec,
        compiler_params=_cparams(("parallel", "arbitrary")),
        name="fox_decode",
    )(page_table, q, k_new, v_new, lf_new, cache_k, cache_v, cache_lft)


def _outproj_kernel(oa_ref, orw_ref, sg_ref, x_ref, woa_ref, wor_ref, wout_ref, nf_ref, wr_ref, br_ref, x1_ref,
                    h2_ref, gate_ref, *, n_experts):
    d = x_ref.shape[1]
    a = _dot(oa_ref[...], woa_ref[...])
    b = _dot(orw_ref[...], wor_ref[...])
    merged = sg_ref[:, :d].astype(F32) * a + sg_ref[:, d:].astype(F32) * b
    x1 = x_ref[...] + _dot(merged.astype(BF16), wout_ref[...])
    x1_ref[...] = x1
    h2 = _rms(x1, nf_ref[...])
    h2_ref[...] = h2.astype(BF16)
    logits = _dot(h2, wr_ref[...], HIGHEST) + br_ref[...]
    lane = lax.broadcasted_iota(jnp.int32, logits.shape, 1)
    work = jnp.where(lane < n_experts, logits, -jnp.inf)
    gate = jnp.zeros_like(logits)
    top = None
    denom = jnp.zeros((logits.shape[0], 1), F32)
    picks = []
    for _ in range(TOP_K):
        mx = jnp.max(work, axis=-1, keepdims=True)
        idx = jnp.min(jnp.where(work == mx, lane, LANES), axis=-1, keepdims=True)
        hit = lane == idx
        top = mx if top is None else top
        e = jnp.exp(mx - top)
        denom = denom + e
        picks.append((hit, e))
        work = jnp.where(hit, -jnp.inf, work)
    for hit, e in picks:
        gate = jnp.where(hit, e / denom, gate)
    gate_ref[...] = gate


def _outproj(o_attn, o_rwkv, sg, x, w_oa, w_or, w_out, norm_ffn, w_router, b_router, n_experts, tm):
    m, d = x.shape
    row = lambda n: pl.BlockSpec((tm, n), lambda i: (i, 0))
    return pl.pallas_call(
        functools.partial(_outproj_kernel, n_experts=n_experts),
        out_shape=[jax.ShapeDtypeStruct((m, d), F32), jax.ShapeDtypeStruct((m, d), BF16),
                   jax.ShapeDtypeStruct((m, LANES), F32)],
        grid=(pl.cdiv(m, tm),),
        in_specs=[row(WIDTH), row(WIDTH), row(2 * d), row(d), _full((WIDTH, d)), _full((WIDTH, d)), _full((d, d)),
                  _full((1, d)), _full((d, LANES)), _full((1, LANES))],
        out_specs=[row(d), row(d), row(LANES)],
        compiler_params=_cparams(("parallel",)),
        name="outproj_router",
    )(o_attn, o_rwkv, sg, x, w_oa, w_or, w_out, norm_ffn, w_router, b_router)


def _moe_kernel(h_ref, gate_ref, x1_ref, wu_ref, bu_ref, wd_ref, bd_ref, nf_ref, y_ref, acc_ref):
    e = pl.program_id(1)
    d_ff = wd_ref.shape[1]

    @pl.when(e == 0)
    def _():
        acc_ref[...] = jnp.zeros_like(acc_ref)

    u = _dot(h_ref[...], wu_ref[0]) + bu_ref[0]
    x_glu = jnp.minimum(u[:, :d_ff], SWIGLU_LIMIT)
    x_lin = jnp.clip(u[:, d_ff:], -SWIGLU_LIMIT, SWIGLU_LIMIT)
    act = x_glu * _sigmoid(SWIGLU_ALPHA * x_glu) * (x_lin + 1.0)
    dn = _dot(act.astype(BF16), wd_ref[0]) + bd_ref[0]
    lane = lax.broadcasted_iota(jnp.int32, gate_ref.shape, 1)
    g = jnp.sum(jnp.where(lane == e, gate_ref[...], 0.0), axis=-1, keepdims=True)
    acc_ref[...] += g * dn

    @pl.when(e == pl.num_programs(1) - 1)
    def _():
        y_ref[...] = _rms(x1_ref[...] + acc_ref[...], nf_ref[...])


def _moe(h2, gate, x1, w_up, b_up, w_down, b_down, norm_final, tm):
    m, d = x1.shape
    n_exp, _, d_up = w_up.shape
    row = lambda n: pl.BlockSpec((tm, n), lambda i, e: (i, 0))
    return pl.pallas_call(
        _moe_kernel,
        out_shape=jax.ShapeDtypeStruct((m, d), F32),
        grid=(pl.cdiv(m, tm), n_exp),
        in_specs=[row(d), row(LANES), row(d),
                  pl.BlockSpec((1, d, d_up), lambda i, e: (e, 0, 0)),
                  pl.BlockSpec((1, 1, d_up), lambda i, e: (e, 0, 0)),
                  pl.BlockSpec((1, w_down.shape[1], d), lambda i, e: (e, 0, 0)),
                  pl.BlockSpec((1, 1, d), lambda i, e: (e, 0, 0)),
                  pl.BlockSpec((1, d), lambda i, e: (0, 0))],
        out_specs=row(d),
        scratch_shapes=[pltpu.VMEM((tm, d), F32)],
        compiler_params=_cparams(("parallel", "arbitrary")),
        name="moe_dense",
    )(h2, gate, x1, w_up, b_up, w_down, b_down, norm_final)


def _pad_cols(a, n):
    return jnp.pad(a, ((0, 0), (0, n - a.shape[1])))


def _pack_rwkv_cols(a):
    o = 3 * WIDTH
    return jnp.concatenate([
        a[:, :o], _pad_cols(a[:, o:o + DECAY_LORA], LANES),
        _pad_cols(a[:, o + DECAY_LORA:o + DECAY_LORA + AAA_LORA], LANES),
        _pad_cols(a[:, o + DECAY_LORA + AAA_LORA:], ZG_COLS)], axis=1)


def _unpack_rwkv_cols(a):
    return jnp.concatenate([a[..., :ZW_OFF], a[..., ZW_OFF:ZW_OFF + DECAY_LORA], a[..., ZA_OFF:ZA_OFF + AAA_LORA],
                            a[..., ZG_OFF:ZG_OFF + GATE_LORA]], axis=-1)


def _pad_rows(a, n):
    return jnp.pad(a, ((0, n - a.shape[0]), (0, 0)))


def kernel(x_prompt, x_sample, cache_k, cache_v, cache_logf, page_table, state_wkv, state_shift, meta_tokens, norm_mix, w_in, b_f, w_oa, rwkv_mu, rwkv_w0, rwkv_w2, rwkv_a0, rwkv_a2, rwkv_g2, rwkv_k_k, rwkv_k_a, rwkv_r_k, rwkv_ln_w, rwkv_ln_b, w_or, w_out, norm_ffn, w_router, b_router, w_up, b_up, w_down, b_down, norm_final):
    b, seq, d = x_prompt.shape
    db = x_sample.shape[0]
    depth = w_in.shape[0]
    assert depth == 1 and x_sample.shape[1] == 1
    length = N_META + seq
    n_exp = w_router.shape[2]
    a_cols = 3 * WIDTH + HEADS
    r_cols = 3 * WIDTH + DECAY_LORA + AAA_LORA + GATE_LORA

    wi = w_in[0]
    w_packed = jnp.concatenate([
        wi[:, :QKV_COLS], _pad_cols(wi[:, QKV_COLS:a_cols], F_COLS),
        _pack_rwkv_cols(wi[:, a_cols:a_cols + r_cols]), wi[:, a_cols + r_cols:]], axis=1).astype(BF16)
    bf_pad = _pad_cols(b_f[0][None, :], F_COLS)
    rp = {
        "mu": _pack_rwkv_cols(rwkv_mu[0][None, :]),
        "w0": rwkv_w0[0][None, :], "a0": rwkv_a0[0][None, :],
        "w2": _pad_rows(rwkv_w2[0], LANES).astype(BF16), "a2": _pad_rows(rwkv_a2[0], LANES).astype(BF16),
        "g2": _pad_rows(rwkv_g2[0], ZG_COLS).astype(BF16),
        "k_k": rwkv_k_k[0][None, :], "k_a": rwkv_k_a[0][None, :], "r_k": rwkv_r_k[0].reshape(1, WIDTH),
        "ln_w": rwkv_ln_w[0][None, :], "ln_b": rwkv_ln_b[0][None, :],
    }
    g_mix = norm_mix[0][None, :]
    g_ffn = norm_ffn[0][None, :]
    g_fin = norm_final[None, :]
    woa, wor, wout = w_oa[0].astype(BF16), w_or[0].astype(BF16), w_out[0].astype(BF16)
    wr_pad = _pad_cols(w_router[0], LANES)
    br_pad = _pad_cols(b_router[0][None, :], LANES)
    wu, wd = w_up[0].astype(BF16), w_down[0].astype(BF16)
    bu, bd = b_up[0][:, None, :], b_down[0][:, None, :]

    meta = jnp.broadcast_to(meta_tokens.astype(x_prompt.dtype)[None], (b, N_META, d))
    xp = jnp.concatenate([meta, x_prompt], axis=1).reshape(b * length, d)
    qb, kb, vb, k_p, v_p, lf_p, zr_p, sg_p = _inproj(xp, g_mix, w_packed, bf_pad, 256)
    c_tok = _cumsum_logf(lf_p.reshape(b, length, F_COLS))
    lpad = pl.cdiv(length, QB) * QB
    c_t = jnp.pad(jnp.transpose(c_tok[:, :, :HEADS], (0, 2, 1)), ((0, 0), (0, 0), (0, lpad - length)))[:, :, None, :]
    as3 = lambda a: a.reshape(b, length, a.shape[-1])
    o_attn = _fox_prompt(as3(qb), as3(kb), as3(vb), c_tok, c_t)
    zr3 = as3(zr_p)
    o_rwkv, wkv_p = _rwkv_prompt(zr3, jnp.zeros((b, 1, R_COLS), F32), rp)
    x1_p, h2_p, gate_p = _outproj(o_attn.reshape(b * length, WIDTH), o_rwkv.reshape(b * length, WIDTH), sg_p, xp,
                                  woa, wor, wout, g_ffn, wr_pad, br_pad, n_exp, 256)
    y_p = _moe(h2_p, gate_p, x1_p, wu, bu, wd, bd, g_fin, 512)

    xs = x_sample.reshape(db, d)
    qs, _, _, k_s, v_s, lf_s, zr_s, sg_s = _inproj(xs, g_mix, w_packed, bf_pad, db)
    n_pool, page = cache_k.shape[1], cache_k.shape[2]
    o_attn_s = _fox_decode(page_table, qs.astype(F32).reshape(db, 1, WIDTH), k_s.reshape(db, 1, WIDTH),
                           v_s.reshape(db, 1, WIDTH), lf_s[:, :HEADS, None],
                           cache_k[0].reshape(n_pool, page, WIDTH), cache_v[0].reshape(n_pool, page, WIDTH),
                           jnp.transpose(cache_logf[0], (0, 2, 1)))
    o_rwkv_s, wkv_s = _rwkv_decode(zr_s, _pack_rwkv_cols(state_shift[0, :, 0, :]), state_wkv[0], rp)
    x1_s, h2_s, gate_s = _outproj(o_attn_s.reshape(db, WIDTH), o_rwkv_s.reshape(db, WIDTH).astype(BF16), sg_s, xs, woa, wor, wout, g_ffn, wr_pad,
                                  br_pad, n_exp, db)
    y_s = _moe(h2_s, gate_s, x1_s, wu, bu, wd, bd, g_fin, db)

    y_prompt = y_p.reshape(b, length, d)[:, N_META:]
    y_sample = y_s.reshape(db, 1, d)
    hd = lambda a, n, t: a.reshape(1, n, t, HEADS, HEAD_DIM)
    logf_prompt = lf_p[:, :HEADS].reshape(1, b, length, HEADS)
    logf_sample = lf_s[:, :HEADS].reshape(1, db, 1, HEADS)
    shift_prompt = _unpack_rwkv_cols(zr3[:, length - 1:length, :])[None]
    shift_sample = _unpack_rwkv_cols(zr_s)[None, :, None, :]
    return (y_prompt, y_sample, hd(k_p, b, length), hd(v_p, b, length), logf_prompt, hd(k_s, db, 1),
            hd(v_s, db, 1), logf_sample, wkv_p[None], shift_prompt, wkv_s[None], shift_sample)
```

```python
import functools
import math

import jax
import jax.numpy as jnp
from jax import lax
from jax.experimental import pallas as pl
from jax.experimental.pallas import tpu as pltpu

F32 = jnp.float32
BF16 = jnp.bfloat16
HIGHEST = lax.Precision.HIGHEST

N_META = 16
HEADS = 8
HEAD_DIM = 64
WIDTH = HEADS * HEAD_DIM
DECAY_LORA = 64
AAA_LORA = 64
GATE_LORA = 160
ATTN_SCALE = 1.0 / math.sqrt(HEAD_DIM)
LOG2E = math.log2(math.e)
NEG_INF = -1e30
GN_EPS = 64e-5
NORM_EPS = 1e-5
TOP_K = 4
SWIGLU_ALPHA = 1.702
SWIGLU_LIMIT = 7.0

LANES = 128
QKV_COLS = 3 * WIDTH
F_OFF = QKV_COLS
F_COLS = LANES
R_OFF = F_OFF + F_COLS
ZW_OFF = 3 * WIDTH
ZA_OFF = ZW_OFF + LANES
ZG_OFF = ZA_OFF + LANES
ZG_COLS = 2 * LANES
R_COLS = ZG_OFF + ZG_COLS
G_OFF = R_OFF + R_COLS
VMEM_LIMIT = 56 * 1024 * 1024


def _cparams(sem):
    return pltpu.CompilerParams(dimension_semantics=sem, vmem_limit_bytes=VMEM_LIMIT)


def _full(shape):
    n = len(shape)
    return pl.BlockSpec(shape, lambda *_: (0,) * n)


def _log_sigmoid(x):
    return jnp.minimum(x, 0.0) - jnp.log1p(jnp.exp(-jnp.abs(x)))


def _sigmoid(x):
    return 1.0 / (1.0 + jnp.exp(-x))


def _rms(x, g):
    return x * lax.rsqrt(jnp.mean(x * x, axis=-1, keepdims=True) + NORM_EPS) * g


def _dot(a, b, prec=None):
    return jnp.dot(a, b, preferred_element_type=F32, precision=prec)


def _dot_nt(a, b, prec=None):
    return lax.dot_general(a, b, (((1,), (1,)), ((), ())), preferred_element_type=F32, precision=prec)


def _dot_tn(a, b):
    return _dot(a.T.astype(BF16), b)


def _split_dot(m, x, terms, left=True):
    acc = None
    for _ in range(terms):
        hi = x.astype(BF16)
        part = _dot(m, hi) if left else _dot(hi, m)
        acc = part if acc is None else acc + part
        x = x - hi.astype(F32)
    return acc


def _inproj_kernel(x_ref, g_ref, w_ref, bf_ref, qb_ref, kb_ref, vb_ref, k_ref, v_ref, lf_ref, zr_ref, sg_ref):
    h = _rms(x_ref[...], g_ref[...]).astype(BF16)

    def mm(lo, n):
        return _dot(h, w_ref[:, lo:lo + n])

    qb_ref[...] = (mm(0, WIDTH) * (ATTN_SCALE * LOG2E)).astype(BF16)
    k = mm(WIDTH, WIDTH)
    k_ref[...] = k
    kb_ref[...] = k.astype(BF16)
    v = mm(2 * WIDTH, WIDTH)
    v_ref[...] = v
    vb_ref[...] = v.astype(BF16)
    lf_ref[...] = _log_sigmoid(mm(F_OFF, F_COLS) + bf_ref[...])
    for c in range(R_COLS // WIDTH):
        zr_ref[:, c * WIDTH:(c + 1) * WIDTH] = mm(R_OFF + c * WIDTH, WIDTH)
    for c in range(2 * 1024 // WIDTH):
        sg_ref[:, c * WIDTH:(c + 1) * WIDTH] = _sigmoid(mm(G_OFF + c * WIDTH, WIDTH)).astype(BF16)


def _inproj(x, g, w_packed, bf_pad, tm):
    m, d = x.shape
    ncols = w_packed.shape[1]
    row = lambda n: pl.BlockSpec((tm, n), lambda i: (i, 0))
    outs = [
        jax.ShapeDtypeStruct((m, WIDTH), BF16), jax.ShapeDtypeStruct((m, WIDTH), BF16),
        jax.ShapeDtypeStruct((m, WIDTH), BF16), jax.ShapeDtypeStruct((m, WIDTH), F32),
        jax.ShapeDtypeStruct((m, WIDTH), F32), jax.ShapeDtypeStruct((m, F_COLS), F32),
        jax.ShapeDtypeStruct((m, R_COLS), F32), jax.ShapeDtypeStruct((m, 2 * d), BF16),
    ]
    return pl.pallas_call(
        _inproj_kernel,
        out_shape=outs,
        grid=(pl.cdiv(m, tm),),
        in_specs=[row(d), _full((1, d)), _full((d, ncols)), _full((1, F_COLS))],
        out_specs=[row(WIDTH), row(WIDTH), row(WIDTH), row(WIDTH), row(WIDTH), row(F_COLS), row(R_COLS),
                   row(2 * d)],
        compiler_params=_cparams(("parallel",)),
        name="inproj",
    )(x, g, w_packed, bf_pad)


CUM_BLOCK = 256


def _cumsum_kernel(lf_ref, c_ref):
    length = lf_ref.shape[1]
    r = lax.broadcasted_iota(jnp.int32, (CUM_BLOCK, CUM_BLOCK), 0)
    c = lax.broadcasted_iota(jnp.int32, (CUM_BLOCK, CUM_BLOCK), 1)
    tri = (c <= r).astype(F32)
    carry = jnp.zeros((1, lf_ref.shape[2]), F32)
    for lo in range(0, length, CUM_BLOCK):
        n = min(CUM_BLOCK, length - lo)
        blk = _dot(tri[:n, :n], lf_ref[0, lo:lo + n, :], HIGHEST) + carry
        c_ref[0, lo:lo + n, :] = blk * LOG2E
        carry = blk[n - 1:n, :]


def _cumsum_logf(lf):
    b, length, n = lf.shape
    spec = pl.BlockSpec((1, length, n), lambda i: (i, 0, 0))
    return pl.pallas_call(
        _cumsum_kernel,
        out_shape=jax.ShapeDtypeStruct(lf.shape, F32),
        grid=(b,),
        in_specs=[spec],
        out_specs=spec,
        compiler_params=_cparams(("parallel",)),
        name="cumsum_logf",
    )(lf)


QB = 256
PAIR = 2


def _attn_kernel(q_ref, k_ref, v_ref, ct_ref, o_ref, *, n_full, tail):
    grp = pl.program_id(1)
    lane_head = lax.broadcasted_iota(jnp.int32, (1, LANES), 1) // HEAD_DIM
    one = jnp.ones((), BF16)

    def attend(qt, q_rows, n_off, diag_k, diag_v, diag_lo):
        qs = jnp.concatenate([jnp.where(lane_head == hh, qt, jnp.zeros_like(qt)) for hh in range(PAIR)], axis=0)

        def step(kt, vt, c0, carry, masked):
            s_all = _dot_nt(qs, kt)
            new = []
            for hh in range(PAIR):
                m, acc = carry[hh]
                s = s_all[hh * q_rows:(hh + 1) * q_rows] - ct_ref[0, grp * PAIR + hh, :, pl.ds(c0, QB)]
                if masked:
                    ri = lax.broadcasted_iota(jnp.int32, (q_rows, QB), 0)
                    ci = lax.broadcasted_iota(jnp.int32, (q_rows, QB), 1)
                    s = jnp.where(ci <= ri, s, NEG_INF)
                m_new = jnp.maximum(m, jnp.max(s, axis=-1, keepdims=True))
                p = jnp.exp2(s - m_new).astype(BF16)
                acc = jnp.exp2(m - m_new) * acc + _dot(p, jnp.where(lane_head == hh, vt, one))
                new.append((m_new, acc))
            return tuple(new)

        def body(j, carry):
            c0 = pl.multiple_of(j * QB, QB)
            return step(k_ref[0, pl.ds(c0, QB), :], v_ref[0, pl.ds(c0, QB), :], c0, carry, False)

        init = tuple((jnp.full((q_rows, 1), NEG_INF, F32), jnp.zeros((q_rows, LANES), F32)) for _ in range(PAIR))
        carry = step(diag_k, diag_v, diag_lo, lax.fori_loop(0, n_off, body, init), True)
        outs = [acc / pltpu.roll(acc, HEAD_DIM, axis=1) for _, acc in carry]
        return jnp.where(lane_head == 0, outs[0], outs[1]).astype(o_ref.dtype)

    def qblock(i, _):
        r0 = pl.multiple_of(i * QB, QB)
        rows = pl.ds(r0, QB)
        o_ref[0, rows, :] = attend(q_ref[0, rows, :], QB, i, k_ref[0, rows, :], v_ref[0, rows, :], r0)
        return 0

    lax.fori_loop(0, n_full, qblock, 0)
    if tail:
        lo = n_full * QB
        pad = jnp.zeros((QB - tail, LANES), k_ref.dtype)
        kd = jnp.concatenate([k_ref[0, lo:lo + tail, :], pad], axis=0)
        vd = jnp.concatenate([v_ref[0, lo:lo + tail, :], pad], axis=0)
        o_ref[0, lo:lo + tail, :] = attend(q_ref[0, lo:lo + tail, :], tail, n_full, kd, vd, lo)


def _fox_prompt(qb, kb, vb, c_t):
    b, length, _ = qb.shape
    n_full, tail = divmod(length, QB)
    assert tail % 16 == 0 and c_t.shape[3] == (n_full + (1 if tail else 0)) * QB
    qkv = pl.BlockSpec((1, length, LANES), lambda i, g: (i, 0, g))
    return pl.pallas_call(
        functools.partial(_attn_kernel, n_full=n_full, tail=tail),
        out_shape=jax.ShapeDtypeStruct(qb.shape, BF16),
        grid=(b, HEADS // PAIR),
        in_specs=[qkv, qkv, qkv, pl.BlockSpec((1, HEADS, 1, c_t.shape[3]), lambda i, g: (i, 0, 0, 0))],
        out_specs=qkv,
        compiler_params=_cparams(("parallel", "arbitrary")),
        name="fox_prompt",
    )(qb, kb, vb, c_t)


def _head_sum_matrix():
    r = lax.broadcasted_iota(jnp.int32, (WIDTH, WIDTH), 0) // HEAD_DIM
    c = lax.broadcasted_iota(jnp.int32, (WIDTH, WIDTH), 1) // HEAD_DIM
    return (r == c).astype(BF16)


def _rwkv_pointwise(zs, p):
    (w0, w2, a0, a2, g2, k_k, k_a, r_k) = p
    r = zs[:, 0:WIDTH]
    k = zs[:, WIDTH:2 * WIDTH]
    v = zs[:, 2 * WIDTH:3 * WIDTH]
    zw = zs[:, ZW_OFF:ZW_OFF + LANES]
    za = zs[:, ZA_OFF:ZA_OFF + LANES]
    zg = zs[:, ZG_OFF:ZG_OFF + ZG_COLS]
    w_log = _log_sigmoid(w0 + _dot(jnp.tanh(zw).astype(BF16), w2)) - 0.5
    logdec = -jnp.exp(w_log)
    a = _sigmoid(a0 + _dot(za.astype(BF16), a2))
    g = _dot(_sigmoid(zg).astype(BF16), g2)
    hs = _head_sum_matrix()
    kk = k * k_k
    norm = jnp.sqrt(_split_dot(hs, kk * kk, 2, left=False))
    kk = kk / jnp.maximum(norm, 1e-12)
    km = k * (1.0 + (a - 1.0) * k_a)
    bonus = _split_dot(hs, r * km * r_k, 2, left=False) * v
    return r, km, v, kk, a, logdec, g, bonus


def _group_norm_out(y, ln_w, ln_b, bonus, g):
    mean = jnp.mean(y, axis=-1, keepdims=True)
    var = jnp.mean(jnp.square(y - mean), axis=-1, keepdims=True)
    yn = (y - mean) * lax.rsqrt(var + GN_EPS) * ln_w + ln_b
    return (yn + bonus) * g


RCHUNK = 128


def _rwkv_chunk_kernel(zr_ref, zp_ref, sh_ref, mu_ref, w0_ref, w2_ref, a0_ref, a2_ref, g2_ref, kk_ref, ka_ref,
                       rk_ref, lnw_ref, lnb_ref, out_ref, wkv_ref, s_ref, *, length):
    ci = pl.program_id(1)
    n = RCHUNK

    @pl.when(ci == 0)
    def _():
        s_ref[...] = jnp.zeros_like(s_ref)

    zr = zr_ref[0]
    row = lax.broadcasted_iota(jnp.int32, (n, 1), 0)
    prev = jnp.where(ci == 0, sh_ref[0], zp_ref[0, 7:8, :])
    zprev = jnp.where(row == 0, prev, pltpu.roll(zr, 1, axis=0))
    valid = (ci * n + row) < length
    zs = jnp.where(valid, zr + (zprev - zr) * mu_ref[...], 0.0)
    params = (w0_ref[...], w2_ref[...], a0_ref[...], a2_ref[...], g2_ref[...], kk_ref[...], ka_ref[...],
              rk_ref[...])
    r, km, v, kk, a, logdec, g, bonus = _rwkv_pointwise(zs, params)
    logdec = jnp.where(valid, logdec, 0.0)

    ri = lax.broadcasted_iota(jnp.int32, (n, n), 0)
    cj = lax.broadcasted_iota(jnp.int32, (n, n), 1)
    incl = cj <= ri
    strict = cj < ri
    eye = (ri == cj).astype(F32)
    eye_h = eye[:HEAD_DIM, :HEAD_DIM]
    logp = _split_dot(incl.astype(BF16), logdec, 3)
    logp_c = logp[n - 1:n, :]
    e_pos = jnp.exp(logp)
    e_neg = jnp.exp(-logp)
    e_end = jnp.exp(logp_c - logp)
    rt_all = r * e_pos
    pc_all = jnp.exp(logp_c)
    heads = range(HEADS)
    sls = [slice(h * HEAD_DIM, (h + 1) * HEAD_DIM) for h in heads]
    cut = lambda x: [x[:, sl].astype(BF16) for sl in sls]
    rt, kt, at, bt = cut(rt_all), cut(km * e_neg), cut(-kk * jnp.exp(logp - logdec)), cut(kk * a * e_neg)
    bh, kh, vh = cut(kk * a * e_end), cut(km * e_end), cut(v)

    m_all = [_dot_nt(jnp.concatenate([at[h], rt[h]], axis=0), jnp.concatenate([bt[h], kt[h]], axis=0)) for h in heads]
    a_ab = [jnp.where(strict, m[:n, :n], 0.0) for m in m_all]
    a_ak = [jnp.where(strict, m[:n, n:], 0.0).astype(BF16) for m in m_all]
    m_rb = [jnp.where(incl, m[n:, :n], 0.0).astype(BF16) for m in m_all]
    m_rk = [jnp.where(incl, m[n:, n:], 0.0).astype(BF16) for m in m_all]
    t_inv = [eye for _ in heads]
    pw = a_ab
    span = 1
    while span < n:
        pt = [_dot(pw[h].astype(BF16), jnp.concatenate([pw[h], t_inv[h]], axis=1).astype(BF16)) for h in heads]
        pw = [x[:, :n] for x in pt]
        t_inv = [t_inv[h] + pt[h][:, n:] for h in heads]
        span *= 2
    t_bf = [x.astype(BF16) for x in t_inv]
    w_m = [_dot(t_bf[h], at[h]) for h in heads]
    akv = [_dot(a_ak[h], vh[h]).astype(BF16) for h in heads]
    u0 = [_dot(t_bf[h], akv[h]) for h in heads]
    w_bf = [x.astype(BF16) for x in w_m]
    u_bf = [x.astype(BF16) for x in u0]
    qp = [rt_all[:, sls[h]] + _dot(m_rb[h], w_bf[h]) for h in heads]
    y0 = [_dot(m_rb[h], u_bf[h]) + _dot(m_rk[h], vh[h]) for h in heads]
    gt = [eye_h * pc_all[:, sls[h]] + _dot_tn(w_m[h], bh[h]) for h in heads]
    et = [_dot_tn(u0[h], bh[h]) + _dot_tn(v[:, sls[h]], kh[h]) for h in heads]
    for h in heads:
        sl = sls[h]
        s_h = s_ref[h].astype(BF16)
        y = _dot_nt(qp[h].astype(BF16), s_h) + y0[h]
        s_ref[h] = _dot(s_h, gt[h].astype(BF16)) + et[h]
        out_ref[0, :, sl] = _group_norm_out(y, lnw_ref[:, sl], lnb_ref[:, sl], bonus[:, sl],
                                            g[:, sl]).astype(out_ref.dtype)

    @pl.when(ci == pl.num_programs(1) - 1)
    def _():
        wkv_ref[0] = s_ref[...]


def _rwkv_prompt(zr, shift0, rp):
    b, length, _ = zr.shape
    n = RCHUNK
    nchunk = pl.cdiv(length, n)
    vec = _full((1, WIDTH))
    return pl.pallas_call(
        functools.partial(_rwkv_chunk_kernel, length=length),
        out_shape=[jax.ShapeDtypeStruct((b, length, WIDTH), BF16),
                   jax.ShapeDtypeStruct((b, HEADS, HEAD_DIM, HEAD_DIM), F32)],
        grid=(b, nchunk),
        in_specs=[pl.BlockSpec((1, n, R_COLS), lambda i, c: (i, c, 0)),
                  pl.BlockSpec((1, 8, R_COLS), lambda i, c: (i, jnp.maximum(c * (n // 8) - 1, 0), 0)),
                  pl.BlockSpec((1, 1, R_COLS), lambda i, c: (i, 0, 0)),
                  _full((1, R_COLS)), vec, _full((LANES, WIDTH)), vec, _full((LANES, WIDTH)),
                  _full((ZG_COLS, WIDTH)), vec, vec, vec, vec, vec],
        out_specs=[pl.BlockSpec((1, n, WIDTH), lambda i, c: (i, c, 0)),
                   pl.BlockSpec((1, HEADS, HEAD_DIM, HEAD_DIM), lambda i, c: (i, 0, 0, 0))],
        scratch_shapes=[pltpu.VMEM((HEADS, HEAD_DIM, HEAD_DIM), F32)],
        compiler_params=_cparams(("parallel", "arbitrary")),
        name="rwkv_prompt",
    )(zr, zr, shift0, rp["mu"], rp["w0"], rp["w2"], rp["a0"], rp["a2"], rp["g2"], rp["k_k"], rp["k_a"],
      rp["r_k"], rp["ln_w"], rp["ln_b"])


DEC_ROWS = 8


def _rwkv_decode_kernel(zr_ref, sh_ref, st_ref, mu_ref, w0_ref, w2_ref, a0_ref, a2_ref, g2_ref, kk_ref, ka_ref,
                        rk_ref, lnw_ref, lnb_ref, out_ref, ns_ref, vec_ref):
    zr = zr_ref[...]
    zs = zr + (sh_ref[...] - zr) * mu_ref[...]
    params = (w0_ref[...], w2_ref[...], a0_ref[...], a2_ref[...], g2_ref[...], kk_ref[...], ka_ref[...],
              rk_ref[...])
    r, km, v, kk, a, logdec, g, bonus = _rwkv_pointwise(zs, params)
    for idx, x in enumerate((r, km, v, kk, kk * a, jnp.exp(logdec), g, bonus)):
        for s in range(DEC_ROWS):
            vec_ref[idx, s] = x[s:s + 1, :]
    ri = lax.broadcasted_iota(jnp.int32, (HEAD_DIM, HEAD_DIM), 0)
    cj = lax.broadcasted_iota(jnp.int32, (HEAD_DIM, HEAD_DIM), 1)
    eye = ri == cj

    def per_row(s, _):
        for h in range(HEADS):
            sl = slice(h * HEAD_DIM, (h + 1) * HEAD_DIM)
            r_h, k_h, v_h, kk_h, ka_h, w_h, g_h, bo_h = (vec_ref[i, s, :, sl] for i in range(8))
            st = st_ref[s, h]
            sa = jnp.sum(st * (-kk_h), axis=-1, keepdims=True)
            v_col = jnp.sum(jnp.where(eye, v_h, 0.0), axis=-1, keepdims=True)
            st = st * w_h + sa * ka_h + v_col * k_h
            ns_ref[s, h] = st
            y_col = jnp.sum(st * r_h, axis=-1, keepdims=True)
            y = jnp.sum(jnp.where(eye, y_col, 0.0), axis=0, keepdims=True)
            out_ref[s, :, sl] = _group_norm_out(y, lnw_ref[:, sl], lnb_ref[:, sl], bo_h, g_h)
        return 0

    lax.fori_loop(0, DEC_ROWS, per_row, 0)


def _rwkv_decode(zr, shift, state, rp):
    n = zr.shape[0]
    vec = _full((1, WIDTH))
    rows = lambda c: pl.BlockSpec((DEC_ROWS, c), lambda i: (i, 0))
    st = pl.BlockSpec((DEC_ROWS, HEADS, HEAD_DIM, HEAD_DIM), lambda i: (i, 0, 0, 0))
    return pl.pallas_call(
        _rwkv_decode_kernel,
        out_shape=[jax.ShapeDtypeStruct((n, 1, WIDTH), F32), jax.ShapeDtypeStruct(state.shape, F32)],
        grid=(n // DEC_ROWS,),
        in_specs=[rows(R_COLS), rows(R_COLS), st, _full((1, R_COLS)), vec, _full((LANES, WIDTH)), vec,
                  _full((LANES, WIDTH)), _full((ZG_COLS, WIDTH)), vec, vec, vec, vec, vec],
        out_specs=[pl.BlockSpec((DEC_ROWS, 1, WIDTH), lambda i: (i, 0, 0)), st],
        scratch_shapes=[pltpu.VMEM((8, DEC_ROWS, 1, WIDTH), F32)],
        compiler_params=_cparams(("parallel",)),
        name="rwkv_decode",
    )(zr, shift, state, rp["mu"], rp["w0"], rp["w2"], rp["a0"], rp["a2"], rp["g2"], rp["k_k"], rp["k_a"],
      rp["r_k"], rp["ln_w"], rp["ln_b"])


SUFFIX_PAGES = 512
DEC_PAGES = 8


def _page_suffix_kernel(lf_ref, e_ref, tot_ref):
    pages, heads, plen = lf_ref.shape
    x = lf_ref[...].reshape(pages * heads, plen)
    u = lax.broadcasted_iota(jnp.int32, (plen, plen), 0)
    t = lax.broadcasted_iota(jnp.int32, (plen, plen), 1)
    after = _split_dot((u > t).astype(BF16), x, 3, left=False)
    total = _split_dot(jnp.ones((plen, plen), BF16), x, 3, left=False)
    e_ref[...] = (after * LOG2E).reshape(pages, heads, plen)
    tot_ref[...] = (total * LOG2E).reshape(pages, heads, plen)


def _page_suffix(lft):
    n_pool, heads, plen = lft.shape
    spec = pl.BlockSpec((SUFFIX_PAGES, heads, plen), lambda i: (i, 0, 0))
    return pl.pallas_call(
        _page_suffix_kernel,
        out_shape=[jax.ShapeDtypeStruct(lft.shape, F32)] * 2,
        grid=(pl.cdiv(n_pool, SUFFIX_PAGES),),
        in_specs=[spec],
        out_specs=[spec, spec],
        compiler_params=_cparams(("parallel",)),
        name="page_suffix",
    )(lft)


def _fox_decode_kernel(pt_ref, q_ref, kn_ref, vn_ref, lfn_ref, *refs):
    del pt_ref
    k_refs, v_refs, e_refs, t_refs = (refs[i * DEC_PAGES:(i + 1) * DEC_PAGES] for i in range(4))
    o_ref, m_ref, l_ref, acc_ref, carry_ref = refs[4 * DEC_PAGES:]
    g = pl.program_id(1)
    rows = k_refs[0].shape[2] * HEADS
    sub = lax.broadcasted_iota(jnp.int32, (HEADS, rows), 0)
    own = (lax.broadcasted_iota(jnp.int32, (HEADS, rows), 1) & (HEADS - 1)) == sub
    qh = q_ref[0]

    @pl.when(g == 0)
    def _():
        m_ref[...] = jnp.full_like(m_ref, NEG_INF)
        l_ref[...] = jnp.zeros_like(l_ref)
        acc_ref[...] = jnp.zeros_like(acc_ref)
        carry_ref[...] = lfn_ref[0] * LOG2E

    carry = carry_ref[...]
    scores = []
    for i in range(DEC_PAGES):
        k2 = k_refs[i][0, 0].reshape(rows, HEAD_DIM).astype(BF16)
        scores.append(jnp.where(own, _dot_nt(qh.astype(BF16), k2) + e_refs[i][0] + carry, NEG_INF))
        carry = carry + t_refs[i][0][:, 0:1]
    carry_ref[...] = carry
    top = scores[0]
    for s in scores[1:]:
        top = jnp.maximum(top, s)
    m_new = jnp.maximum(m_ref[...], jnp.max(top, axis=-1, keepdims=True))
    alpha = jnp.exp2(m_ref[...] - m_new)
    m_ref[...] = m_new
    psum = None
    pv = None
    for i in range(DEC_PAGES):
        pr = jnp.exp2(scores[i] - m_new)
        part = _dot(pr.astype(BF16), v_refs[i][0, 0].reshape(rows, HEAD_DIM).astype(BF16))
        psum = pr if psum is None else psum + pr
        pv = part if pv is None else pv + part
    l_ref[...] = alpha * l_ref[...] + jnp.sum(psum, axis=-1, keepdims=True)
    acc_ref[...] = alpha * acc_ref[...] + pv

    @pl.when(g == pl.num_programs(1) - 1)
    def _():
        s_new = jnp.sum(qh * kn_ref[0], axis=-1, keepdims=True)
        m_fin = jnp.maximum(m_ref[...], s_new)
        a_fin = jnp.exp2(m_ref[...] - m_fin)
        p_new = jnp.exp2(s_new - m_fin)
        o_ref[0] = (a_fin * acc_ref[...] + p_new * vn_ref[0]) / (a_fin * l_ref[...] + p_new)


def _fox_decode(page_table, q, k_new, v_new, lf_new, cache_k, cache_v, e_flat, tot):
    n, n_pages = page_table.shape
    page = cache_k.shape[2]
    assert n_pages % DEC_PAGES == 0
    tok = pl.BlockSpec((1, HEADS, HEAD_DIM), lambda s, g, pt: (s, 0, 0))

    def paged(shape, i, lead):
        idx = lambda s, g, pt: lead + (pt[s, n_pages - 1 - (g * DEC_PAGES + i)],) + (0,) * (len(shape) - len(lead) - 1)
        return pl.BlockSpec(shape, idx)

    slots = range(DEC_PAGES)
    kv_shape = (1, 1, page, HEADS, HEAD_DIM)
    grid_spec = pltpu.PrefetchScalarGridSpec(
        num_scalar_prefetch=1,
        grid=(n, n_pages // DEC_PAGES),
        in_specs=[tok, tok, tok, pl.BlockSpec((1, HEADS, 1), lambda s, g, pt: (s, 0, 0))]
        + [paged(kv_shape, i, (0,)) for i in slots] + [paged(kv_shape, i, (0,)) for i in slots]
        + [paged((1, 1, page * HEADS), i, ()) for i in slots] + [paged((1, HEADS, page), i, ()) for i in slots],
        out_specs=tok,
        scratch_shapes=[pltpu.VMEM((HEADS, 1), F32), pltpu.VMEM((HEADS, 1), F32),
                        pltpu.VMEM((HEADS, HEAD_DIM), F32), pltpu.VMEM((HEADS, 1), F32)],
    )
    return pl.pallas_call(
        _fox_decode_kernel,
        out_shape=jax.ShapeDtypeStruct((n, HEADS, HEAD_DIM), F32),
        grid_spec=grid_spec,
        compiler_params=_cparams(("parallel", "arbitrary")),
        name="fox_decode",
    )(page_table, q, k_new, v_new, lf_new, *([cache_k] * DEC_PAGES), *([cache_v] * DEC_PAGES),
      *([e_flat] * DEC_PAGES), *([tot] * DEC_PAGES))


def _outproj_kernel(oa_ref, orw_ref, sg_ref, x_ref, woa_ref, wor_ref, wout_ref, nf_ref, wrt_ref, brc_ref, x1_ref,
                    h2t_ref, gate_ref, gate3_ref, *, n_valid):
    i = pl.program_id(0)
    d = x_ref.shape[1]
    n_exp = wrt_ref.shape[0]

    @pl.when(i < n_valid)
    def _():
        a = _dot(oa_ref[...], woa_ref[...])
        b = _dot(orw_ref[...], wor_ref[...])
        merged = sg_ref[:, :d].astype(F32) * a + sg_ref[:, d:].astype(F32) * b
        x1 = x_ref[...] + _dot(merged.astype(BF16), wout_ref[...])
        x1_ref[...] = x1
        h2 = _rms(x1, nf_ref[...])
        h2t_ref[...] = h2.T.astype(BF16)
        work = _dot_nt(wrt_ref[...], h2, HIGHEST) + brc_ref[...]
        row = lax.broadcasted_iota(jnp.int32, work.shape, 0)
        top = None
        denom = jnp.zeros((1, work.shape[1]), F32)
        picks = []
        for _ in range(TOP_K):
            mx = jnp.max(work, axis=0, keepdims=True)
            idx = jnp.min(jnp.where(work == mx, row, n_exp), axis=0, keepdims=True)
            hit = row == idx
            top = mx if top is None else top
            e = jnp.exp(mx - top)
            denom = denom + e
            picks.append((hit, e))
            work = jnp.where(hit, -jnp.inf, work)
        gate = jnp.zeros(work.shape, F32)
        for hit, e in picks:
            gate = jnp.where(hit, e / denom, gate)
        gate_ref[...] = gate
        for ex in range(n_exp):
            gate3_ref[ex] = gate[ex:ex + 1, :]

    @pl.when(i >= n_valid)
    def _():
        x1_ref[...] = jnp.zeros_like(x1_ref)
        h2t_ref[...] = jnp.zeros_like(h2t_ref)
        gate_ref[...] = jnp.zeros_like(gate_ref)
        gate3_ref[...] = jnp.zeros_like(gate3_ref)


def _outproj(o_attn, o_rwkv, sg, x, w_oa, w_or, w_out, norm_ffn, w_router_t, b_router_col, tm, m_pad):
    m, d = x.shape
    n_exp = w_router_t.shape[0]
    n_valid = m // tm
    assert m % tm == 0 and m_pad % tm == 0
    last = n_valid - 1
    row = lambda n: pl.BlockSpec((tm, n), lambda i: (jnp.minimum(i, last), 0))
    return pl.pallas_call(
        functools.partial(_outproj_kernel, n_valid=n_valid),
        out_shape=[jax.ShapeDtypeStruct((m_pad, d), F32), jax.ShapeDtypeStruct((d, m_pad), BF16),
                   jax.ShapeDtypeStruct((n_exp, m_pad), F32), jax.ShapeDtypeStruct((n_exp, 1, m_pad), F32)],
        grid=(m_pad // tm,),
        in_specs=[row(WIDTH), row(WIDTH), row(2 * d), row(d), _full((WIDTH, d)), _full((WIDTH, d)), _full((d, d)),
                  _full((1, d)), _full((n_exp, d)), _full((n_exp, 1))],
        out_specs=[pl.BlockSpec((tm, d), lambda i: (i, 0)), pl.BlockSpec((d, tm), lambda i: (0, i)),
                   pl.BlockSpec((n_exp, tm), lambda i: (0, i)), pl.BlockSpec((n_exp, 1, tm), lambda i: (0, 0, i))],
        compiler_params=_cparams(("parallel",)),
        name="outproj_router",
    )(o_attn, o_rwkv, sg, x, w_oa, w_or, w_out, norm_ffn, w_router_t, b_router_col)


MOE_TOKENS = 1792
MOE_SLOTS = 256
MOE_FF_CHUNK = 256


def _moe_kernel(h2t_ref, gate_ref, gate3_ref, wu_ref, bu_ref, wd_ref, bd_ref, o_ref, acc_ref, rank_ref):
    e = pl.program_id(1)
    d, tt = h2t_ref.shape
    n_exp = gate_ref.shape[0]
    d_ff = wd_ref.shape[2]

    @pl.when(e == 0)
    def _():
        acc_ref[...] = jnp.zeros_like(acc_ref)
        tr = lax.broadcasted_iota(jnp.int32, (tt, tt), 0)
        tc = lax.broadcasted_iota(jnp.int32, (tt, tt), 1)
        hit_all = jnp.where(gate_ref[...] > 0.0, 1.0, 0.0).astype(BF16)
        rank = _dot(hit_all, (tr < tc).astype(BF16))
        for ex in range(n_exp):
            rank_ref[ex] = rank[ex:ex + 1, :]

    g_row = gate3_ref[e]
    hit = g_row > 0.0
    slot_of = jnp.where(hit, rank_ref[e], -1.0)
    count = jnp.sum(jnp.where(hit, 1.0, 0.0)).astype(jnp.int32)
    reps = MOE_SLOTS // LANES
    cols = lambda ref, lo, n: jnp.concatenate([ref[0, lo:lo + n, :]] * reps, axis=1)

    def one_pass(blk, _):
        slot = (blk * MOE_SLOTS + lax.broadcasted_iota(jnp.int32, (MOE_SLOTS, 1), 0)).astype(F32)
        sel = slot_of == slot
        sel_one = jnp.where(sel, 1.0, 0.0).astype(BF16)
        sel_gate = jnp.where(sel, g_row, 0.0).astype(BF16)
        xg = _dot_nt(h2t_ref[...], sel_one).astype(BF16)
        dn = None
        for lo in range(0, d_ff, MOE_FF_CHUNK):
            x_glu = _dot(wu_ref[0, lo:lo + MOE_FF_CHUNK, :], xg) + cols(bu_ref, lo, MOE_FF_CHUNK)
            x_lin = _dot(wu_ref[0, d_ff + lo:d_ff + lo + MOE_FF_CHUNK, :], xg) + cols(bu_ref, d_ff + lo, MOE_FF_CHUNK)
            x_glu = jnp.minimum(x_glu, SWIGLU_LIMIT)
            x_lin = jnp.clip(x_lin, -SWIGLU_LIMIT, SWIGLU_LIMIT)
            act = (x_glu * _sigmoid(SWIGLU_ALPHA * x_glu) * (x_lin + 1.0)).astype(BF16)
            part = _dot(wd_ref[0, :, lo:lo + MOE_FF_CHUNK], act)
            dn = part if dn is None else dn + part
        for lo in range(0, d, MOE_FF_CHUNK):
            rows = slice(lo, lo + MOE_FF_CHUNK)
            acc_ref[rows, :] += _dot((dn[rows] + cols(bd_ref, lo, MOE_FF_CHUNK)).astype(BF16), sel_gate)
        return 0

    lax.fori_loop(0, (count + MOE_SLOTS - 1) // MOE_SLOTS, one_pass, 0)

    @pl.when(e == pl.num_programs(1) - 1)
    def _():
        o_ref[...] = acc_ref[...].T.astype(o_ref.dtype)


def _moe(h2t, gate, gate3, w_up_t, b_up, w_down_t, b_down, tt):
    d, m_pad = h2t.shape
    n_exp, d_up, _ = w_up_t.shape
    assert m_pad % tt == 0
    return pl.pallas_call(
        _moe_kernel,
        out_shape=jax.ShapeDtypeStruct((m_pad, d), BF16),
        grid=(m_pad // tt, n_exp),
        in_specs=[pl.BlockSpec((d, tt), lambda i, e: (0, i)),
                  pl.BlockSpec((n_exp, tt), lambda i, e: (0, i)),
                  pl.BlockSpec((n_exp, 1, tt), lambda i, e: (0, 0, i)),
                  pl.BlockSpec((1, d_up, d), lambda i, e: (e, 0, 0)),
                  pl.BlockSpec((1, d_up, LANES), lambda i, e: (e, 0, 0)),
                  pl.BlockSpec((1, d, w_down_t.shape[2]), lambda i, e: (e, 0, 0)),
                  pl.BlockSpec((1, d, LANES), lambda i, e: (e, 0, 0))],
        out_specs=pl.BlockSpec((tt, d), lambda i, e: (i, 0)),
        scratch_shapes=[pltpu.VMEM((d, tt), F32), pltpu.VMEM((n_exp, 1, tt), F32)],
        compiler_params=_cparams(("parallel", "arbitrary")),
        name="moe_routed",
    )(h2t, gate, gate3, w_up_t, b_up, w_down_t, b_down)


def _final_kernel(x1_ref, moe_ref, g_ref, y_ref):
    y_ref[...] = _rms(x1_ref[...] + moe_ref[...].astype(F32), g_ref[...])


def _final_norm(x1, moe, g, m, tm):
    d = x1.shape[1]
    row = pl.BlockSpec((tm, d), lambda i: (i, 0))
    return pl.pallas_call(
        _final_kernel,
        out_shape=jax.ShapeDtypeStruct((m, d), F32),
        grid=(pl.cdiv(m, tm),),
        in_specs=[row, row, _full((1, d))],
        out_specs=row,
        compiler_params=_cparams(("parallel",)),
        name="final_norm",
    )(x1, moe, g)


def _pad_cols(a, n):
    return jnp.pad(a, ((0, 0), (0, n - a.shape[1])))


def _pack_rwkv_cols(a):
    o = 3 * WIDTH
    return jnp.concatenate([
        a[:, :o], _pad_cols(a[:, o:o + DECAY_LORA], LANES),
        _pad_cols(a[:, o + DECAY_LORA:o + DECAY_LORA + AAA_LORA], LANES),
        _pad_cols(a[:, o + DECAY_LORA + AAA_LORA:], ZG_COLS)], axis=1)


def _unpack_rwkv_cols(a):
    return jnp.concatenate([a[..., :ZW_OFF], a[..., ZW_OFF:ZW_OFF + DECAY_LORA], a[..., ZA_OFF:ZA_OFF + AAA_LORA],
                            a[..., ZG_OFF:ZG_OFF + GATE_LORA]], axis=-1)


def _pad_rows(a, n):
    return jnp.pad(a, ((0, n - a.shape[0]), (0, 0)))


def kernel(x_prompt, x_sample, cache_k, cache_v, cache_logf, page_table, state_wkv, state_shift, meta_tokens, norm_mix, w_in, b_f, w_oa, rwkv_mu, rwkv_w0, rwkv_w2, rwkv_a0, rwkv_a2, rwkv_g2, rwkv_k_k, rwkv_k_a, rwkv_r_k, rwkv_ln_w, rwkv_ln_b, w_or, w_out, norm_ffn, w_router, b_router, w_up, b_up, w_down, b_down, norm_final):
    b, seq, d = x_prompt.shape
    db = x_sample.shape[0]
    depth = w_in.shape[0]
    assert depth == 1 and x_sample.shape[1] == 1
    length = N_META + seq
    n_exp = w_router.shape[2]
    a_cols = 3 * WIDTH + HEADS
    r_cols = 3 * WIDTH + DECAY_LORA + AAA_LORA + GATE_LORA

    wi = w_in[0]
    w_packed = jnp.concatenate([
        wi[:, :QKV_COLS], _pad_cols(wi[:, QKV_COLS:a_cols], F_COLS),
        _pack_rwkv_cols(wi[:, a_cols:a_cols + r_cols]), wi[:, a_cols + r_cols:]], axis=1).astype(BF16)
    bf_pad = _pad_cols(b_f[0][None, :], F_COLS)
    rp = {
        "mu": _pack_rwkv_cols(rwkv_mu[0][None, :]),
        "w0": rwkv_w0[0][None, :], "a0": rwkv_a0[0][None, :],
        "w2": _pad_rows(rwkv_w2[0], LANES).astype(BF16), "a2": _pad_rows(rwkv_a2[0], LANES).astype(BF16),
        "g2": _pad_rows(rwkv_g2[0], ZG_COLS).astype(BF16),
        "k_k": rwkv_k_k[0][None, :], "k_a": rwkv_k_a[0][None, :], "r_k": rwkv_r_k[0].reshape(1, WIDTH),
        "ln_w": rwkv_ln_w[0][None, :], "ln_b": rwkv_ln_b[0][None, :],
    }
    g_mix = norm_mix[0][None, :]
    g_ffn = norm_ffn[0][None, :]
    g_fin = norm_final[None, :]
    woa, wor, wout = w_oa[0].astype(BF16), w_or[0].astype(BF16), w_out[0].astype(BF16)
    wr_t = jnp.transpose(w_router[0])
    br_col = b_router[0][:, None]
    wu_t = jnp.transpose(w_up[0], (0, 2, 1)).astype(BF16)
    wd_t = jnp.transpose(w_down[0], (0, 2, 1)).astype(BF16)
    bu = jnp.broadcast_to(b_up[0][:, :, None], b_up.shape[1:] + (LANES,))
    bd = jnp.broadcast_to(b_down[0][:, :, None], b_down.shape[1:] + (LANES,))

    meta = jnp.broadcast_to(meta_tokens.astype(x_prompt.dtype)[None], (b, N_META, d))
    m_p = b * length
    xp = jnp.concatenate([meta, x_prompt], axis=1).reshape(m_p, d)
    qb, kb, vb, k_p, v_p, lf_p, zr_p, sg_p = _inproj(xp, g_mix, w_packed, bf_pad, 256)
    c2 = _cumsum_logf(lf_p.reshape(b, length, F_COLS))
    lpad = pl.cdiv(length, QB) * QB
    c_t = jnp.pad(jnp.transpose(c2[:, :, :HEADS], (0, 2, 1)), ((0, 0), (0, 0), (0, lpad - length)))[:, :, None, :]
    as3 = lambda a: a.reshape(b, length, a.shape[-1])
    o_attn = _fox_prompt(as3(qb), as3(kb), as3(vb), c_t)
    zr3 = as3(zr_p)
    o_rwkv, wkv_p = _rwkv_prompt(zr3, jnp.zeros((b, 1, R_COLS), F32), rp)
    m_pad = pl.cdiv(m_p, MOE_TOKENS) * MOE_TOKENS
    x1_p, h2t_p, gate_p, gate3_p = _outproj(o_attn.reshape(m_p, WIDTH), o_rwkv.reshape(m_p, WIDTH), sg_p, xp, woa,
                                            wor, wout, g_ffn, wr_t, br_col, 256, m_pad)
    moe_p = _moe(h2t_p, gate_p, gate3_p, wu_t, bu, wd_t, bd, MOE_TOKENS)
    y_p = _final_norm(x1_p, moe_p, g_fin, m_p, 512)

    xs = x_sample.reshape(db, d)
    qs, _, _, k_s, v_s, lf_s, zr_s, sg_s = _inproj(xs, g_mix, w_packed, bf_pad, db)
    hd3 = lambda a: a.reshape(db, HEADS, HEAD_DIM)
    n_pool, page = cache_k.shape[1], cache_k.shape[2]
    e_page, tot_page = _page_suffix(jnp.transpose(cache_logf[0], (0, 2, 1)))
    e_flat = jnp.transpose(e_page, (0, 2, 1)).reshape(n_pool, 1, page * HEADS)
    o_attn_s = _fox_decode(page_table, hd3(qs.astype(F32)), hd3(k_s), hd3(v_s), lf_s[:, :HEADS, None], cache_k,
                           cache_v, e_flat, tot_page)
    o_rwkv_s, wkv_s = _rwkv_decode(zr_s, _pack_rwkv_cols(state_shift[0, :, 0, :]), state_wkv[0], rp)
    x1_s, h2t_s, gate_s, gate3_s = _outproj(o_attn_s.reshape(db, WIDTH).astype(BF16),
                                            o_rwkv_s.reshape(db, WIDTH).astype(BF16), sg_s, xs, woa, wor, wout, g_ffn,
                                            wr_t, br_col, db, db)
    moe_s = _moe(h2t_s, gate_s, gate3_s, wu_t, bu, wd_t, bd, db)
    y_s = _final_norm(x1_s, moe_s, g_fin, db, db)

    y_prompt = y_p.reshape(b, length, d)[:, N_META:]
    y_sample = y_s.reshape(db, 1, d)
    hd = lambda a, n, t: a.reshape(1, n, t, HEADS, HEAD_DIM)
    logf_prompt = lf_p[:, :HEADS].reshape(1, b, length, HEADS)
    logf_sample = lf_s[:, :HEADS].reshape(1, db, 1, HEADS)
    shift_prompt = _unpack_rwkv_cols(zr3[:, length - 1:length, :])[None]
    shift_sample = _unpack_rwkv_cols(zr_s)[None, :, None, :]
    return (y_prompt, y_sample, hd(k_p, b, length), hd(v_p, b, length), logf_prompt, hd(k_s, db, 1),
            hd(v_s, db, 1), logf_sample, wkv_p[None], shift_prompt, wkv_s[None], shift_sample)
```

```python
import functools
import math

import jax
import jax.numpy as jnp
from jax import lax
from jax.experimental import pallas as pl
from jax.experimental.pallas import tpu as pltpu

F32 = jnp.float32
BF16 = jnp.bfloat16
HIGHEST = lax.Precision.HIGHEST

N_META = 16
HEADS = 8
HEAD_DIM = 64
WIDTH = HEADS * HEAD_DIM
DECAY_LORA = 64
AAA_LORA = 64
GATE_LORA = 160
ATTN_SCALE = 1.0 / math.sqrt(HEAD_DIM)
LOG2E = math.log2(math.e)
NEG_INF = -1e30
GN_EPS = 64e-5
NORM_EPS = 1e-5
TOP_K = 4
SWIGLU_ALPHA = 1.702
SWIGLU_LIMIT = 7.0

LANES = 128
QKV_COLS = 3 * WIDTH
F_OFF = QKV_COLS
F_COLS = LANES
R_OFF = F_OFF + F_COLS
ZW_OFF = 3 * WIDTH
ZA_OFF = ZW_OFF + LANES
ZG_OFF = ZA_OFF + LANES
ZG_COLS = 2 * LANES
R_COLS = ZG_OFF + ZG_COLS
G_OFF = R_OFF + R_COLS
VMEM_LIMIT = 56 * 1024 * 1024


def _cparams(sem):
    return pltpu.CompilerParams(dimension_semantics=sem, vmem_limit_bytes=VMEM_LIMIT)


def _full(shape):
    n = len(shape)
    return pl.BlockSpec(shape, lambda *_: (0,) * n)


def _log_sigmoid(x):
    return jnp.minimum(x, 0.0) - jnp.log1p(jnp.exp(-jnp.abs(x)))


def _sigmoid(x):
    return 1.0 / (1.0 + jnp.exp(-x))


def _rms(x, g):
    return x * lax.rsqrt(jnp.mean(x * x, axis=-1, keepdims=True) + NORM_EPS) * g


def _dot(a, b, prec=None):
    return jnp.dot(a, b, preferred_element_type=F32, precision=prec)


def _dot_nt(a, b, prec=None):
    return lax.dot_general(a, b, (((1,), (1,)), ((), ())), preferred_element_type=F32, precision=prec)


def _dot_tn(a, b):
    return _dot(a.T.astype(BF16), b)


def _split_dot(m, x, terms, left=True):
    acc = None
    for _ in range(terms):
        hi = x.astype(BF16)
        part = _dot(m, hi) if left else _dot(hi, m)
        acc = part if acc is None else acc + part
        x = x - hi.astype(F32)
    return acc


def _inproj_kernel(x_ref, g_ref, w_ref, bf_ref, qb_ref, kb_ref, vb_ref, k_ref, v_ref, lf_ref, zr_ref, sg_ref):
    h = _rms(x_ref[...], g_ref[...]).astype(BF16)

    def mm(lo, n):
        return _dot(h, w_ref[:, lo:lo + n])

    qb_ref[...] = (mm(0, WIDTH) * (ATTN_SCALE * LOG2E)).astype(BF16)
    k = mm(WIDTH, WIDTH)
    k_ref[...] = k
    kb_ref[...] = k.astype(BF16)
    v = mm(2 * WIDTH, WIDTH)
    v_ref[...] = v
    vb_ref[...] = v.astype(BF16)
    lf_ref[...] = _log_sigmoid(mm(F_OFF, F_COLS) + bf_ref[...])
    for c in range(R_COLS // WIDTH):
        zr_ref[:, c * WIDTH:(c + 1) * WIDTH] = mm(R_OFF + c * WIDTH, WIDTH)
    for c in range(2 * 1024 // WIDTH):
        sg_ref[:, c * WIDTH:(c + 1) * WIDTH] = _sigmoid(mm(G_OFF + c * WIDTH, WIDTH)).astype(BF16)


def _inproj(x, g, w_packed, bf_pad, tm):
    m, d = x.shape
    ncols = w_packed.shape[1]
    row = lambda n: pl.BlockSpec((tm, n), lambda i: (i, 0))
    outs = [
        jax.ShapeDtypeStruct((m, WIDTH), BF16), jax.ShapeDtypeStruct((m, WIDTH), BF16),
        jax.ShapeDtypeStruct((m, WIDTH), BF16), jax.ShapeDtypeStruct((m, WIDTH), F32),
        jax.ShapeDtypeStruct((m, WIDTH), F32), jax.ShapeDtypeStruct((m, F_COLS), F32),
        jax.ShapeDtypeStruct((m, R_COLS), F32), jax.ShapeDtypeStruct((m, 2 * d), BF16),
    ]
    return pl.pallas_call(
        _inproj_kernel,
        out_shape=outs,
        grid=(pl.cdiv(m, tm),),
        in_specs=[row(d), _full((1, d)), _full((d, ncols)), _full((1, F_COLS))],
        out_specs=[row(WIDTH), row(WIDTH), row(WIDTH), row(WIDTH), row(WIDTH), row(F_COLS), row(R_COLS),
                   row(2 * d)],
        compiler_params=_cparams(("parallel",)),
        name="inproj",
    )(x, g, w_packed, bf_pad)


CUM_BLOCK = 256


def _cumsum_kernel(lf_ref, c_ref):
    length = lf_ref.shape[1]
    r = lax.broadcasted_iota(jnp.int32, (CUM_BLOCK, CUM_BLOCK), 0)
    c = lax.broadcasted_iota(jnp.int32, (CUM_BLOCK, CUM_BLOCK), 1)
    tri = (c <= r).astype(F32)
    carry = jnp.zeros((1, lf_ref.shape[2]), F32)
    for lo in range(0, length, CUM_BLOCK):
        n = min(CUM_BLOCK, length - lo)
        blk = _dot(tri[:n, :n], lf_ref[0, lo:lo + n, :], HIGHEST) + carry
        c_ref[0, lo:lo + n, :] = blk * LOG2E
        carry = blk[n - 1:n, :]


def _cumsum_logf(lf):
    b, length, n = lf.shape
    spec = pl.BlockSpec((1, length, n), lambda i: (i, 0, 0))
    return pl.pallas_call(
        _cumsum_kernel,
        out_shape=jax.ShapeDtypeStruct(lf.shape, F32),
        grid=(b,),
        in_specs=[spec],
        out_specs=spec,
        compiler_params=_cparams(("parallel",)),
        name="cumsum_logf",
    )(lf)


QB = 256
PAIR = 2


def _attn_kernel(q_ref, k_ref, v_ref, ct_ref, o_ref, *, n_full, tail):
    grp = pl.program_id(1)
    lane_head = lax.broadcasted_iota(jnp.int32, (1, LANES), 1) // HEAD_DIM
    one = jnp.ones((), BF16)

    def attend(qt, q_rows, n_off, diag_k, diag_v, diag_lo):
        qs = jnp.concatenate([jnp.where(lane_head == hh, qt, jnp.zeros_like(qt)) for hh in range(PAIR)], axis=0)

        def step(kt, vt, c0, carry, masked):
            s_all = _dot_nt(qs, kt)
            new = []
            for hh in range(PAIR):
                m, acc = carry[hh]
                s = s_all[hh * q_rows:(hh + 1) * q_rows] - ct_ref[0, grp * PAIR + hh, :, pl.ds(c0, QB)]
                if masked:
                    ri = lax.broadcasted_iota(jnp.int32, (q_rows, QB), 0)
                    ci = lax.broadcasted_iota(jnp.int32, (q_rows, QB), 1)
                    s = jnp.where(ci <= ri, s, NEG_INF)
                m_new = jnp.maximum(m, jnp.max(s, axis=-1, keepdims=True))
                p = jnp.exp2(s - m_new).astype(BF16)
                acc = jnp.exp2(m - m_new) * acc + _dot(p, jnp.where(lane_head == hh, vt, one))
                new.append((m_new, acc))
            return tuple(new)

        def body(j, carry):
            c0 = pl.multiple_of(j * QB, QB)
            return step(k_ref[0, pl.ds(c0, QB), :], v_ref[0, pl.ds(c0, QB), :], c0, carry, False)

        init = tuple((jnp.full((q_rows, 1), NEG_INF, F32), jnp.zeros((q_rows, LANES), F32)) for _ in range(PAIR))
        carry = step(diag_k, diag_v, diag_lo, lax.fori_loop(0, n_off, body, init), True)
        outs = [acc / pltpu.roll(acc, HEAD_DIM, axis=1) for _, acc in carry]
        return jnp.where(lane_head == 0, outs[0], outs[1]).astype(o_ref.dtype)

    def qblock(i, _):
        r0 = pl.multiple_of(i * QB, QB)
        rows = pl.ds(r0, QB)
        o_ref[0, rows, :] = attend(q_ref[0, rows, :], QB, i, k_ref[0, rows, :], v_ref[0, rows, :], r0)
        return 0

    lax.fori_loop(0, n_full, qblock, 0)
    if tail:
        lo = n_full * QB
        pad = jnp.zeros((QB - tail, LANES), k_ref.dtype)
        kd = jnp.concatenate([k_ref[0, lo:lo + tail, :], pad], axis=0)
        vd = jnp.concatenate([v_ref[0, lo:lo + tail, :], pad], axis=0)
        o_ref[0, lo:lo + tail, :] = attend(q_ref[0, lo:lo + tail, :], tail, n_full, kd, vd, lo)


def _fox_prompt(qb, kb, vb, c_t):
    b, length, _ = qb.shape
    n_full, tail = divmod(length, QB)
    assert tail % 16 == 0 and c_t.shape[3] == (n_full + (1 if tail else 0)) * QB
    qkv = pl.BlockSpec((1, length, LANES), lambda i, g: (i, 0, g))
    return pl.pallas_call(
        functools.partial(_attn_kernel, n_full=n_full, tail=tail),
        out_shape=jax.ShapeDtypeStruct(qb.shape, BF16),
        grid=(b, HEADS // PAIR),
        in_specs=[qkv, qkv, qkv, pl.BlockSpec((1, HEADS, 1, c_t.shape[3]), lambda i, g: (i, 0, 0, 0))],
        out_specs=qkv,
        compiler_params=_cparams(("parallel", "arbitrary")),
        name="fox_prompt",
    )(qb, kb, vb, c_t)


def _head_sum_matrix():
    r = lax.broadcasted_iota(jnp.int32, (WIDTH, WIDTH), 0) // HEAD_DIM
    c = lax.broadcasted_iota(jnp.int32, (WIDTH, WIDTH), 1) // HEAD_DIM
    return (r == c).astype(BF16)


def _rwkv_pointwise(zs, p):
    (w0, w2, a0, a2, g2, k_k, k_a, r_k) = p
    r = zs[:, 0:WIDTH]
    k = zs[:, WIDTH:2 * WIDTH]
    v = zs[:, 2 * WIDTH:3 * WIDTH]
    zw = zs[:, ZW_OFF:ZW_OFF + LANES]
    za = zs[:, ZA_OFF:ZA_OFF + LANES]
    zg = zs[:, ZG_OFF:ZG_OFF + ZG_COLS]
    w_log = _log_sigmoid(w0 + _dot(jnp.tanh(zw).astype(BF16), w2)) - 0.5
    logdec = -jnp.exp(w_log)
    a = _sigmoid(a0 + _dot(za.astype(BF16), a2))
    g = _dot(_sigmoid(zg).astype(BF16), g2)
    hs = _head_sum_matrix()
    kk = k * k_k
    norm = jnp.sqrt(_split_dot(hs, kk * kk, 2, left=False))
    kk = kk / jnp.maximum(norm, 1e-12)
    km = k * (1.0 + (a - 1.0) * k_a)
    bonus = _split_dot(hs, r * km * r_k, 2, left=False) * v
    return r, km, v, kk, a, logdec, g, bonus


def _group_norm_out(y, ln_w, ln_b, bonus, g):
    mean = jnp.mean(y, axis=-1, keepdims=True)
    var = jnp.mean(jnp.square(y - mean), axis=-1, keepdims=True)
    yn = (y - mean) * lax.rsqrt(var + GN_EPS) * ln_w + ln_b
    return (yn + bonus) * g


RCHUNK = 128


def _rwkv_chunk_kernel(zr_ref, zp_ref, sh_ref, mu_ref, w0_ref, w2_ref, a0_ref, a2_ref, g2_ref, kk_ref, ka_ref,
                       rk_ref, lnw_ref, lnb_ref, out_ref, wkv_ref, s_ref, *, length):
    ci = pl.program_id(1)
    n = RCHUNK

    @pl.when(ci == 0)
    def _():
        s_ref[...] = jnp.zeros_like(s_ref)

    zr = zr_ref[0]
    row = lax.broadcasted_iota(jnp.int32, (n, 1), 0)
    prev = jnp.where(ci == 0, sh_ref[0], zp_ref[0, 7:8, :])
    zprev = jnp.where(row == 0, prev, pltpu.roll(zr, 1, axis=0))
    valid = (ci * n + row) < length
    zs = jnp.where(valid, zr + (zprev - zr) * mu_ref[...], 0.0)
    params = (w0_ref[...], w2_ref[...], a0_ref[...], a2_ref[...], g2_ref[...], kk_ref[...], ka_ref[...],
              rk_ref[...])
    r, km, v, kk, a, logdec, g, bonus = _rwkv_pointwise(zs, params)
    logdec = jnp.where(valid, logdec, 0.0)

    ri = lax.broadcasted_iota(jnp.int32, (n, n), 0)
    cj = lax.broadcasted_iota(jnp.int32, (n, n), 1)
    incl = cj <= ri
    strict = cj < ri
    eye = (ri == cj).astype(F32)
    eye_h = eye[:HEAD_DIM, :HEAD_DIM]
    logp = _split_dot(incl.astype(BF16), logdec, 3)
    logp_c = logp[n - 1:n, :]
    e_pos = jnp.exp(logp)
    e_neg = jnp.exp(-logp)
    e_end = jnp.exp(logp_c - logp)
    rt_all = r * e_pos
    pc_all = jnp.exp(logp_c)
    heads = range(HEADS)
    sls = [slice(h * HEAD_DIM, (h + 1) * HEAD_DIM) for h in heads]
    cut = lambda x: [x[:, sl].astype(BF16) for sl in sls]
    rt, kt, at, bt = cut(rt_all), cut(km * e_neg), cut(-kk * jnp.exp(logp - logdec)), cut(kk * a * e_neg)
    bh, kh, vh = cut(kk * a * e_end), cut(km * e_end), cut(v)

    m_all = [_dot_nt(jnp.concatenate([at[h], rt[h]], axis=0), jnp.concatenate([bt[h], kt[h]], axis=0)) for h in heads]
    a_ab = [jnp.where(strict, m[:n, :n], 0.0) for m in m_all]
    a_ak = [jnp.where(strict, m[:n, n:], 0.0).astype(BF16) for m in m_all]
    m_rb = [jnp.where(incl, m[n:, :n], 0.0).astype(BF16) for m in m_all]
    m_rk = [jnp.where(incl, m[n:, n:], 0.0).astype(BF16) for m in m_all]
    t_inv = [eye for _ in heads]
    pw = a_ab
    span = 1
    while span < n:
        pt = [_dot(pw[h].astype(BF16), jnp.concatenate([pw[h], t_inv[h]], axis=1).astype(BF16)) for h in heads]
        pw = [x[:, :n] for x in pt]
        t_inv = [t_inv[h] + pt[h][:, n:] for h in heads]
        span *= 2
    t_bf = [x.astype(BF16) for x in t_inv]
    w_m = [_dot(t_bf[h], at[h]) for h in heads]
    akv = [_dot(a_ak[h], vh[h]).astype(BF16) for h in heads]
    u0 = [_dot(t_bf[h], akv[h]) for h in heads]
    w_bf = [x.astype(BF16) for x in w_m]
    u_bf = [x.astype(BF16) for x in u0]
    qp = [rt_all[:, sls[h]] + _dot(m_rb[h], w_bf[h]) for h in heads]
    y0 = [_dot(m_rb[h], u_bf[h]) + _dot(m_rk[h], vh[h]) for h in heads]
    gt = [eye_h * pc_all[:, sls[h]] + _dot_tn(w_m[h], bh[h]) for h in heads]
    et = [_dot_tn(u0[h], bh[h]) + _dot_tn(v[:, sls[h]], kh[h]) for h in heads]
    for h in heads:
        sl = sls[h]
        s_h = s_ref[h].astype(BF16)
        y = _dot_nt(qp[h].astype(BF16), s_h) + y0[h]
        s_ref[h] = _dot(s_h, gt[h].astype(BF16)) + et[h]
        out_ref[0, :, sl] = _group_norm_out(y, lnw_ref[:, sl], lnb_ref[:, sl], bonus[:, sl],
                                            g[:, sl]).astype(out_ref.dtype)

    @pl.when(ci == pl.num_programs(1) - 1)
    def _():
        wkv_ref[0] = s_ref[...]


def _rwkv_prompt(zr, shift0, rp):
    b, length, _ = zr.shape
    n = RCHUNK
    nchunk = pl.cdiv(length, n)
    vec = _full((1, WIDTH))
    return pl.pallas_call(
        functools.partial(_rwkv_chunk_kernel, length=length),
        out_shape=[jax.ShapeDtypeStruct((b, length, WIDTH), BF16),
                   jax.ShapeDtypeStruct((b, HEADS, HEAD_DIM, HEAD_DIM), F32)],
        grid=(b, nchunk),
        in_specs=[pl.BlockSpec((1, n, R_COLS), lambda i, c: (i, c, 0)),
                  pl.BlockSpec((1, 8, R_COLS), lambda i, c: (i, jnp.maximum(c * (n // 8) - 1, 0), 0)),
                  pl.BlockSpec((1, 1, R_COLS), lambda i, c: (i, 0, 0)),
                  _full((1, R_COLS)), vec, _full((LANES, WIDTH)), vec, _full((LANES, WIDTH)),
                  _full((ZG_COLS, WIDTH)), vec, vec, vec, vec, vec],
        out_specs=[pl.BlockSpec((1, n, WIDTH), lambda i, c: (i, c, 0)),
                   pl.BlockSpec((1, HEADS, HEAD_DIM, HEAD_DIM), lambda i, c: (i, 0, 0, 0))],
        scratch_shapes=[pltpu.VMEM((HEADS, HEAD_DIM, HEAD_DIM), F32)],
        compiler_params=_cparams(("parallel", "arbitrary")),
        name="rwkv_prompt",
    )(zr, zr, shift0, rp["mu"], rp["w0"], rp["w2"], rp["a0"], rp["a2"], rp["g2"], rp["k_k"], rp["k_a"],
      rp["r_k"], rp["ln_w"], rp["ln_b"])


DEC_ROWS = 8


def _rwkv_decode_kernel(zr_ref, sh_ref, st_ref, mu_ref, w0_ref, w2_ref, a0_ref, a2_ref, g2_ref, kk_ref, ka_ref,
                        rk_ref, lnw_ref, lnb_ref, out_ref, ns_ref, vec_ref):
    zr = zr_ref[...]
    zs = zr + (sh_ref[...] - zr) * mu_ref[...]
    params = (w0_ref[...], w2_ref[...], a0_ref[...], a2_ref[...], g2_ref[...], kk_ref[...], ka_ref[...],
              rk_ref[...])
    r, km, v, kk, a, logdec, g, bonus = _rwkv_pointwise(zs, params)
    for idx, x in enumerate((r, km, v, kk, kk * a, jnp.exp(logdec), g, bonus)):
        for s in range(DEC_ROWS):
            vec_ref[idx, s] = x[s:s + 1, :]
    ri = lax.broadcasted_iota(jnp.int32, (HEAD_DIM, HEAD_DIM), 0)
    cj = lax.broadcasted_iota(jnp.int32, (HEAD_DIM, HEAD_DIM), 1)
    eye = ri == cj

    def per_row(s, _):
        for h in range(HEADS):
            sl = slice(h * HEAD_DIM, (h + 1) * HEAD_DIM)
            r_h, k_h, v_h, kk_h, ka_h, w_h, g_h, bo_h = (vec_ref[i, s, :, sl] for i in range(8))
            st = st_ref[s, h]
            sa = jnp.sum(st * (-kk_h), axis=-1, keepdims=True)
            v_col = jnp.sum(jnp.where(eye, v_h, 0.0), axis=-1, keepdims=True)
            st = st * w_h + sa * ka_h + v_col * k_h
            ns_ref[s, h] = st
            y_col = jnp.sum(st * r_h, axis=-1, keepdims=True)
            y = jnp.sum(jnp.where(eye, y_col, 0.0), axis=0, keepdims=True)
            out_ref[s, :, sl] = _group_norm_out(y, lnw_ref[:, sl], lnb_ref[:, sl], bo_h, g_h)
        return 0

    lax.fori_loop(0, DEC_ROWS, per_row, 0)


def _rwkv_decode(zr, shift, state, rp):
    n = zr.shape[0]
    vec = _full((1, WIDTH))
    rows = lambda c: pl.BlockSpec((DEC_ROWS, c), lambda i: (i, 0))
    st = pl.BlockSpec((DEC_ROWS, HEADS, HEAD_DIM, HEAD_DIM), lambda i: (i, 0, 0, 0))
    return pl.pallas_call(
        _rwkv_decode_kernel,
        out_shape=[jax.ShapeDtypeStruct((n, 1, WIDTH), F32), jax.ShapeDtypeStruct(state.shape, F32)],
        grid=(n // DEC_ROWS,),
        in_specs=[rows(R_COLS), rows(R_COLS), st, _full((1, R_COLS)), vec, _full((LANES, WIDTH)), vec,
                  _full((LANES, WIDTH)), _full((ZG_COLS, WIDTH)), vec, vec, vec, vec, vec],
        out_specs=[pl.BlockSpec((DEC_ROWS, 1, WIDTH), lambda i: (i, 0, 0)), st],
        scratch_shapes=[pltpu.VMEM((8, DEC_ROWS, 1, WIDTH), F32)],
        compiler_params=_cparams(("parallel",)),
        name="rwkv_decode",
    )(zr, shift, state, rp["mu"], rp["w0"], rp["w2"], rp["a0"], rp["a2"], rp["g2"], rp["k_k"], rp["k_a"],
      rp["r_k"], rp["ln_w"], rp["ln_b"])


SUFFIX_PAGES = 512
DEC_PAGES = 8


def _page_suffix_kernel(lf_ref, e_ref, tot_ref):
    pages, heads, plen = lf_ref.shape
    x = lf_ref[...].reshape(pages * heads, plen)
    u = lax.broadcasted_iota(jnp.int32, (plen, plen), 0)
    t = lax.broadcasted_iota(jnp.int32, (plen, plen), 1)
    after = _split_dot((u > t).astype(BF16), x, 3, left=False)
    total = _split_dot(jnp.ones((plen, plen), BF16), x, 3, left=False)
    e_ref[...] = (after * LOG2E).reshape(pages, heads, plen)
    tot_ref[...] = (total * LOG2E).reshape(pages, heads, plen)


def _page_suffix(lft):
    n_pool, heads, plen = lft.shape
    spec = pl.BlockSpec((SUFFIX_PAGES, heads, plen), lambda i: (i, 0, 0))
    return pl.pallas_call(
        _page_suffix_kernel,
        out_shape=[jax.ShapeDtypeStruct(lft.shape, F32)] * 2,
        grid=(pl.cdiv(n_pool, SUFFIX_PAGES),),
        in_specs=[spec],
        out_specs=[spec, spec],
        compiler_params=_cparams(("parallel",)),
        name="page_suffix",
    )(lft)


def _fox_decode_kernel(pt_ref, q_ref, kn_ref, vn_ref, lfn_ref, *refs):
    del pt_ref
    k_refs, v_refs, e_refs, t_refs = (refs[i * DEC_PAGES:(i + 1) * DEC_PAGES] for i in range(4))
    o_ref, m_ref, l_ref, acc_ref, carry_ref = refs[4 * DEC_PAGES:]
    g = pl.program_id(1)
    plen = k_refs[0].shape[4]

    @pl.when(g == 0)
    def _():
        m_ref[...] = jnp.full_like(m_ref, NEG_INF)
        l_ref[...] = jnp.zeros_like(l_ref)
        acc_ref[...] = jnp.zeros_like(acc_ref)
        carry_ref[...] = lfn_ref[0] * LOG2E

    for h in range(HEADS):
        row = slice(h, h + 1)
        q_b = jnp.broadcast_to(q_ref[0, h], (HEAD_DIM, plen))
        carry = carry_ref[row, :]
        scores = []
        for i in range(DEC_PAGES):
            s = jnp.sum(k_refs[i][0, 0, h] * q_b, axis=0, keepdims=True)
            scores.append(s + e_refs[i][0, row, :] + carry)
            carry = carry + t_refs[i][0, row, 0:1]
        carry_ref[row, :] = carry
        top = scores[0]
        for s in scores[1:]:
            top = jnp.maximum(top, s)
        m_old = m_ref[row, :]
        m_new = jnp.maximum(m_old, jnp.max(top, axis=-1, keepdims=True))
        alpha = jnp.exp2(m_old - m_new)
        m_ref[row, :] = m_new
        psum = None
        pv = None
        for i in range(DEC_PAGES):
            pr = jnp.exp2(scores[i] - m_new)
            part = v_refs[i][0, 0, h] * pr
            psum = pr if psum is None else psum + pr
            pv = part if pv is None else pv + part
        l_ref[row, :] = alpha * l_ref[row, :] + jnp.sum(psum, axis=-1, keepdims=True)
        acc_ref[h] = alpha * acc_ref[h] + pv

    @pl.when(g == pl.num_programs(1) - 1)
    def _():
        for h in range(HEADS):
            row = slice(h, h + 1)
            s_new = jnp.sum(q_ref[0, h] * kn_ref[0, h], axis=0, keepdims=True)
            m_fin = jnp.maximum(m_ref[row, :], s_new)
            a_fin = jnp.exp2(m_ref[row, :] - m_fin)
            p_new = jnp.exp2(s_new - m_fin)
            num = a_fin * jnp.sum(acc_ref[h], axis=-1, keepdims=True) + p_new * vn_ref[0, h]
            o_ref[0, h] = num / (a_fin * l_ref[row, :] + p_new)


def _fox_decode(page_table, q, k_new, v_new, lf_new, cache_kt, cache_vt, e_page, tot):
    n, n_pages = page_table.shape
    page = cache_kt.shape[4]
    assert n_pages % DEC_PAGES == 0
    tok = pl.BlockSpec((1, HEADS, HEAD_DIM, 1), lambda s, g, pt: (s, 0, 0, 0))

    def paged(shape, i, lead):
        idx = lambda s, g, pt: lead + (pt[s, n_pages - 1 - (g * DEC_PAGES + i)],) + (0,) * (len(shape) - len(lead) - 1)
        return pl.BlockSpec(shape, idx)

    slots = range(DEC_PAGES)
    kv_shape = (1, 1, HEADS, HEAD_DIM, page)
    grid_spec = pltpu.PrefetchScalarGridSpec(
        num_scalar_prefetch=1,
        grid=(n, n_pages // DEC_PAGES),
        in_specs=[tok, tok, tok, pl.BlockSpec((1, HEADS, 1), lambda s, g, pt: (s, 0, 0))]
        + [paged(kv_shape, i, (0,)) for i in slots] + [paged(kv_shape, i, (0,)) for i in slots]
        + [paged((1, HEADS, page), i, ()) for i in slots] + [paged((1, HEADS, page), i, ()) for i in slots],
        out_specs=tok,
        scratch_shapes=[pltpu.VMEM((HEADS, 1), F32), pltpu.VMEM((HEADS, 1), F32),
                        pltpu.VMEM((HEADS, HEAD_DIM, page), F32), pltpu.VMEM((HEADS, 1), F32)],
    )
    return pl.pallas_call(
        _fox_decode_kernel,
        out_shape=jax.ShapeDtypeStruct((n, HEADS, HEAD_DIM, 1), F32),
        grid_spec=grid_spec,
        compiler_params=_cparams(("parallel", "arbitrary")),
        name="fox_decode",
    )(page_table, q, k_new, v_new, lf_new, *([cache_kt] * DEC_PAGES), *([cache_vt] * DEC_PAGES),
      *([e_page] * DEC_PAGES), *([tot] * DEC_PAGES))


def _outproj_kernel(oa_ref, orw_ref, sg_ref, x_ref, woa_ref, wor_ref, wout_ref, nf_ref, wrt_ref, brc_ref, x1_ref,
                    h2t_ref, gate_ref, gate3_ref, *, n_valid):
    i = pl.program_id(0)
    d = x_ref.shape[1]
    n_exp = wrt_ref.shape[0]

    @pl.when(i < n_valid)
    def _():
        a = _dot(oa_ref[...], woa_ref[...])
        b = _dot(orw_ref[...], wor_ref[...])
        merged = sg_ref[:, :d].astype(F32) * a + sg_ref[:, d:].astype(F32) * b
        x1 = x_ref[...] + _dot(merged.astype(BF16), wout_ref[...])
        x1_ref[...] = x1
        h2 = _rms(x1, nf_ref[...])
        h2t_ref[...] = h2.T.astype(BF16)
        work = _dot_nt(wrt_ref[...], h2, HIGHEST) + brc_ref[...]
        row = lax.broadcasted_iota(jnp.int32, work.shape, 0)
        top = None
        denom = jnp.zeros((1, work.shape[1]), F32)
        picks = []
        for _ in range(TOP_K):
            mx = jnp.max(work, axis=0, keepdims=True)
            idx = jnp.min(jnp.where(work == mx, row, n_exp), axis=0, keepdims=True)
            hit = row == idx
            top = mx if top is None else top
            e = jnp.exp(mx - top)
            denom = denom + e
            picks.append((hit, e))
            work = jnp.where(hit, -jnp.inf, work)
        gate = jnp.zeros(work.shape, F32)
        for hit, e in picks:
            gate = jnp.where(hit, e / denom, gate)
        gate_ref[...] = gate
        for ex in range(n_exp):
            gate3_ref[ex] = gate[ex:ex + 1, :]

    @pl.when(i >= n_valid)
    def _():
        x1_ref[...] = jnp.zeros_like(x1_ref)
        h2t_ref[...] = jnp.zeros_like(h2t_ref)
        gate_ref[...] = jnp.zeros_like(gate_ref)
        gate3_ref[...] = jnp.zeros_like(gate3_ref)


def _outproj(o_attn, o_rwkv, sg, x, w_oa, w_or, w_out, norm_ffn, w_router_t, b_router_col, tm, m_pad):
    m, d = x.shape
    n_exp = w_router_t.shape[0]
    n_valid = m // tm
    assert m % tm == 0 and m_pad % tm == 0
    last = n_valid - 1
    row = lambda n: pl.BlockSpec((tm, n), lambda i: (jnp.minimum(i, last), 0))
    return pl.pallas_call(
        functools.partial(_outproj_kernel, n_valid=n_valid),
        out_shape=[jax.ShapeDtypeStruct((m_pad, d), F32), jax.ShapeDtypeStruct((d, m_pad), BF16),
                   jax.ShapeDtypeStruct((n_exp, m_pad), F32), jax.ShapeDtypeStruct((n_exp, 1, m_pad), F32)],
        grid=(m_pad // tm,),
        in_specs=[row(WIDTH), row(WIDTH), row(2 * d), row(d), _full((WIDTH, d)), _full((WIDTH, d)), _full((d, d)),
                  _full((1, d)), _full((n_exp, d)), _full((n_exp, 1))],
        out_specs=[pl.BlockSpec((tm, d), lambda i: (i, 0)), pl.BlockSpec((d, tm), lambda i: (0, i)),
                   pl.BlockSpec((n_exp, tm), lambda i: (0, i)), pl.BlockSpec((n_exp, 1, tm), lambda i: (0, 0, i))],
        compiler_params=_cparams(("parallel",)),
        name="outproj_router",
    )(o_attn, o_rwkv, sg, x, w_oa, w_or, w_out, norm_ffn, w_router_t, b_router_col)


MOE_TOKENS = 1792
MOE_SLOTS = 256
MOE_FF_CHUNK = 256


def _moe_kernel(h2t_ref, gate_ref, gate3_ref, wu_ref, bu_ref, wd_ref, bd_ref, o_ref, acc_ref, rank_ref):
    e = pl.program_id(1)
    d, tt = h2t_ref.shape
    n_exp = gate_ref.shape[0]
    d_ff = wd_ref.shape[2]

    @pl.when(e == 0)
    def _():
        acc_ref[...] = jnp.zeros_like(acc_ref)
        tr = lax.broadcasted_iota(jnp.int32, (tt, tt), 0)
        tc = lax.broadcasted_iota(jnp.int32, (tt, tt), 1)
        hit_all = jnp.where(gate_ref[...] > 0.0, 1.0, 0.0).astype(BF16)
        rank = _dot(hit_all, (tr < tc).astype(BF16))
        for ex in range(n_exp):
            rank_ref[ex] = rank[ex:ex + 1, :]

    g_row = gate3_ref[e]
    hit = g_row > 0.0
    slot_of = jnp.where(hit, rank_ref[e], -1.0)
    count = jnp.sum(jnp.where(hit, 1.0, 0.0)).astype(jnp.int32)
    reps = MOE_SLOTS // LANES
    cols = lambda ref, lo, n: jnp.concatenate([ref[0, lo:lo + n, :]] * reps, axis=1)

    def one_pass(blk, _):
        slot = (blk * MOE_SLOTS + lax.broadcasted_iota(jnp.int32, (MOE_SLOTS, 1), 0)).astype(F32)
        sel = slot_of == slot
        sel_one = jnp.where(sel, 1.0, 0.0).astype(BF16)
        sel_gate = jnp.where(sel, g_row, 0.0).astype(BF16)
        xg = _dot_nt(h2t_ref[...], sel_one).astype(BF16)
        dn = None
        for lo in range(0, d_ff, MOE_FF_CHUNK):
            x_glu = _dot(wu_ref[0, lo:lo + MOE_FF_CHUNK, :], xg) + cols(bu_ref, lo, MOE_FF_CHUNK)
            x_lin = _dot(wu_ref[0, d_ff + lo:d_ff + lo + MOE_FF_CHUNK, :], xg) + cols(bu_ref, d_ff + lo, MOE_FF_CHUNK)
            x_glu = jnp.minimum(x_glu, SWIGLU_LIMIT)
            x_lin = jnp.clip(x_lin, -SWIGLU_LIMIT, SWIGLU_LIMIT)
            act = (x_glu * _sigmoid(SWIGLU_ALPHA * x_glu) * (x_lin + 1.0)).astype(BF16)
            part = _dot(wd_ref[0, :, lo:lo + MOE_FF_CHUNK], act)
            dn = part if dn is None else dn + part
        for lo in range(0, d, MOE_FF_CHUNK):
            rows = slice(lo, lo + MOE_FF_CHUNK)
            acc_ref[rows, :] += _dot((dn[rows] + cols(bd_ref, lo, MOE_FF_CHUNK)).astype(BF16), sel_gate)
        return 0

    lax.fori_loop(0, (count + MOE_SLOTS - 1) // MOE_SLOTS, one_pass, 0)

    @pl.when(e == pl.num_programs(1) - 1)
    def _():
        o_ref[...] = acc_ref[...].T.astype(o_ref.dtype)


def _moe(h2t, gate, gate3, w_up_t, b_up, w_down_t, b_down, tt):
    d, m_pad = h2t.shape
    n_exp, d_up, _ = w_up_t.shape
    assert m_pad % tt == 0
    return pl.pallas_call(
        _moe_kernel,
        out_shape=jax.ShapeDtypeStruct((m_pad, d), BF16),
        grid=(m_pad // tt, n_exp),
        in_specs=[pl.BlockSpec((d, tt), lambda i, e: (0, i)),
                  pl.BlockSpec((n_exp, tt), lambda i, e: (0, i)),
                  pl.BlockSpec((n_exp, 1, tt), lambda i, e: (0, 0, i)),
                  pl.BlockSpec((1, d_up, d), lambda i, e: (e, 0, 0)),
                  pl.BlockSpec((1, d_up, LANES), lambda i, e: (e, 0, 0)),
                  pl.BlockSpec((1, d, w_down_t.shape[2]), lambda i, e: (e, 0, 0)),
                  pl.BlockSpec((1, d, LANES), lambda i, e: (e, 0, 0))],
        out_specs=pl.BlockSpec((tt, d), lambda i, e: (i, 0)),
        scratch_shapes=[pltpu.VMEM((d, tt), F32), pltpu.VMEM((n_exp, 1, tt), F32)],
        compiler_params=_cparams(("parallel", "arbitrary")),
        name="moe_routed",
    )(h2t, gate, gate3, w_up_t, b_up, w_down_t, b_down)


def _final_kernel(x1_ref, moe_ref, g_ref, y_ref):
    y_ref[...] = _rms(x1_ref[...] + moe_ref[...].astype(F32), g_ref[...])


def _final_norm(x1, moe, g, m, tm):
    d = x1.shape[1]
    row = pl.BlockSpec((tm, d), lambda i: (i, 0))
    return pl.pallas_call(
        _final_kernel,
        out_shape=jax.ShapeDtypeStruct((m, d), F32),
        grid=(pl.cdiv(m, tm),),
        in_specs=[row, row, _full((1, d))],
        out_specs=row,
        compiler_params=_cparams(("parallel",)),
        name="final_norm",
    )(x1, moe, g)


def _pad_cols(a, n):
    return jnp.pad(a, ((0, 0), (0, n - a.shape[1])))


def _pack_rwkv_cols(a):
    o = 3 * WIDTH
    return jnp.concatenate([
        a[:, :o], _pad_cols(a[:, o:o + DECAY_LORA], LANES),
        _pad_cols(a[:, o + DECAY_LORA:o + DECAY_LORA + AAA_LORA], LANES),
        _pad_cols(a[:, o + DECAY_LORA + AAA_LORA:], ZG_COLS)], axis=1)


def _unpack_rwkv_cols(a):
    return jnp.concatenate([a[..., :ZW_OFF], a[..., ZW_OFF:ZW_OFF + DECAY_LORA], a[..., ZA_OFF:ZA_OFF + AAA_LORA],
                            a[..., ZG_OFF:ZG_OFF + GATE_LORA]], axis=-1)


def _pad_rows(a, n):
    return jnp.pad(a, ((0, n - a.shape[0]), (0, 0)))


def kernel(x_prompt, x_sample, cache_k, cache_v, cache_logf, page_table, state_wkv, state_shift, meta_tokens, norm_mix, w_in, b_f, w_oa, rwkv_mu, rwkv_w0, rwkv_w2, rwkv_a0, rwkv_a2, rwkv_g2, rwkv_k_k, rwkv_k_a, rwkv_r_k, rwkv_ln_w, rwkv_ln_b, w_or, w_out, norm_ffn, w_router, b_router, w_up, b_up, w_down, b_down, norm_final):
    b, seq, d = x_prompt.shape
    db = x_sample.shape[0]
    depth = w_in.shape[0]
    assert depth == 1 and x_sample.shape[1] == 1
    length = N_META + seq
    n_exp = w_router.shape[2]
    a_cols = 3 * WIDTH + HEADS
    r_cols = 3 * WIDTH + DECAY_LORA + AAA_LORA + GATE_LORA

    wi = w_in[0]
    w_packed = jnp.concatenate([
        wi[:, :QKV_COLS], _pad_cols(wi[:, QKV_COLS:a_cols], F_COLS),
        _pack_rwkv_cols(wi[:, a_cols:a_cols + r_cols]), wi[:, a_cols + r_cols:]], axis=1).astype(BF16)
    bf_pad = _pad_cols(b_f[0][None, :], F_COLS)
    rp = {
        "mu": _pack_rwkv_cols(rwkv_mu[0][None, :]),
        "w0": rwkv_w0[0][None, :], "a0": rwkv_a0[0][None, :],
        "w2": _pad_rows(rwkv_w2[0], LANES).astype(BF16), "a2": _pad_rows(rwkv_a2[0], LANES).astype(BF16),
        "g2": _pad_rows(rwkv_g2[0], ZG_COLS).astype(BF16),
        "k_k": rwkv_k_k[0][None, :], "k_a": rwkv_k_a[0][None, :], "r_k": rwkv_r_k[0].reshape(1, WIDTH),
        "ln_w": rwkv_ln_w[0][None, :], "ln_b": rwkv_ln_b[0][None, :],
    }
    g_mix = norm_mix[0][None, :]
    g_ffn = norm_ffn[0][None, :]
    g_fin = norm_final[None, :]
    woa, wor, wout = w_oa[0].astype(BF16), w_or[0].astype(BF16), w_out[0].astype(BF16)
    wr_t = jnp.transpose(w_router[0])
    br_col = b_router[0][:, None]
    wu_t = jnp.transpose(w_up[0], (0, 2, 1)).astype(BF16)
    wd_t = jnp.transpose(w_down[0], (0, 2, 1)).astype(BF16)
    bu = jnp.broadcast_to(b_up[0][:, :, None], b_up.shape[1:] + (LANES,))
    bd = jnp.broadcast_to(b_down[0][:, :, None], b_down.shape[1:] + (LANES,))

    meta = jnp.broadcast_to(meta_tokens.astype(x_prompt.dtype)[None], (b, N_META, d))
    m_p = b * length
    xp = jnp.concatenate([meta, x_prompt], axis=1).reshape(m_p, d)
    qb, kb, vb, k_p, v_p, lf_p, zr_p, sg_p = _inproj(xp, g_mix, w_packed, bf_pad, 256)
    c2 = _cumsum_logf(lf_p.reshape(b, length, F_COLS))
    lpad = pl.cdiv(length, QB) * QB
    c_t = jnp.pad(jnp.transpose(c2[:, :, :HEADS], (0, 2, 1)), ((0, 0), (0, 0), (0, lpad - length)))[:, :, None, :]
    as3 = lambda a: a.reshape(b, length, a.shape[-1])
    o_attn = _fox_prompt(as3(qb), as3(kb), as3(vb), c_t)
    zr3 = as3(zr_p)
    o_rwkv, wkv_p = _rwkv_prompt(zr3, jnp.zeros((b, 1, R_COLS), F32), rp)
    m_pad = pl.cdiv(m_p, MOE_TOKENS) * MOE_TOKENS
    x1_p, h2t_p, gate_p, gate3_p = _outproj(o_attn.reshape(m_p, WIDTH), o_rwkv.reshape(m_p, WIDTH), sg_p, xp, woa,
                                            wor, wout, g_ffn, wr_t, br_col, 256, m_pad)
    moe_p = _moe(h2t_p, gate_p, gate3_p, wu_t, bu, wd_t, bd, MOE_TOKENS)
    y_p = _final_norm(x1_p, moe_p, g_fin, m_p, 512)

    xs = x_sample.reshape(db, d)
    qs, _, _, k_s, v_s, lf_s, zr_s, sg_s = _inproj(xs, g_mix, w_packed, bf_pad, db)
    col = lambda a: a.reshape(db, HEADS, HEAD_DIM, 1)
    to_lanes = lambda c: jnp.transpose(c, (0, 1, 3, 4, 2))
    e_page, tot_page = _page_suffix(jnp.transpose(cache_logf[0], (0, 2, 1)))
    o_attn_s = _fox_decode(page_table, col(qs.astype(F32)), col(k_s), col(v_s), lf_s[:, :HEADS, None],
                           to_lanes(cache_k), to_lanes(cache_v), e_page, tot_page)
    o_rwkv_s, wkv_s = _rwkv_decode(zr_s, _pack_rwkv_cols(state_shift[0, :, 0, :]), state_wkv[0], rp)
    x1_s, h2t_s, gate_s, gate3_s = _outproj(o_attn_s.reshape(db, WIDTH).astype(BF16),
                                            o_rwkv_s.reshape(db, WIDTH).astype(BF16), sg_s, xs, woa, wor, wout, g_ffn,
                                            wr_t, br_col, db, db)
    moe_s = _moe(h2t_s, gate_s, gate3_s, wu_t, bu, wd_t, bd, db)
    y_s = _final_norm(x1_s, moe_s, g_fin, db, db)

    y_prompt = y_p.reshape(b, length, d)[:, N_META:]
    y_sample = y_s.reshape(db, 1, d)
    hd = lambda a, n, t: a.reshape(1, n, t, HEADS, HEAD_DIM)
    logf_prompt = lf_p[:, :HEADS].reshape(1, b, length, HEADS)
    logf_sample = lf_s[:, :HEADS].reshape(1, db, 1, HEADS)
    shift_prompt = _unpack_rwkv_cols(zr3[:, length - 1:length, :])[None]
    shift_sample = _unpack_rwkv_cols(zr_s)[None, :, None, :]
    return (y_prompt, y_sample, hd(k_p, b, length), hd(v_p, b, length), logf_prompt, hd(k_s, db, 1),
            hd(v_s, db, 1), logf_sample, wkv_p[None], shift_prompt, wkv_s[None], shift_sample)
```

```python
import functools
import math

import jax
import jax.numpy as jnp
from jax import lax
from jax.experimental import pallas as pl
from jax.experimental.pallas import tpu as pltpu

F32 = jnp.float32
BF16 = jnp.bfloat16
HIGHEST = lax.Precision.HIGHEST

N_META = 16
HEADS = 8
HEAD_DIM = 64
WIDTH = HEADS * HEAD_DIM
DECAY_LORA = 64
AAA_LORA = 64
GATE_LORA = 160
ATTN_SCALE = 1.0 / math.sqrt(HEAD_DIM)
LOG2E = math.log2(math.e)
NEG_INF = -1e30
GN_EPS = 64e-5
NORM_EPS = 1e-5
TOP_K = 4
SWIGLU_ALPHA = 1.702
SWIGLU_LIMIT = 7.0

LANES = 128
QKV_COLS = 3 * WIDTH
F_OFF = QKV_COLS
F_COLS = LANES
R_OFF = F_OFF + F_COLS
ZW_OFF = 3 * WIDTH
ZA_OFF = ZW_OFF + LANES
ZG_OFF = ZA_OFF + LANES
ZG_COLS = 2 * LANES
R_COLS = ZG_OFF + ZG_COLS
G_OFF = R_OFF + R_COLS
VMEM_LIMIT = 56 * 1024 * 1024


def _cparams(sem):
    return pltpu.CompilerParams(dimension_semantics=sem, vmem_limit_bytes=VMEM_LIMIT)


def _full(shape):
    n = len(shape)
    return pl.BlockSpec(shape, lambda *_: (0,) * n)


def _log_sigmoid(x):
    return jnp.minimum(x, 0.0) - jnp.log1p(jnp.exp(-jnp.abs(x)))


def _sigmoid(x):
    return 1.0 / (1.0 + jnp.exp(-x))


def _rms(x, g):
    return x * lax.rsqrt(jnp.mean(x * x, axis=-1, keepdims=True) + NORM_EPS) * g


def _dot(a, b, prec=None):
    return jnp.dot(a, b, preferred_element_type=F32, precision=prec)


def _dot_nt(a, b, prec=None):
    return lax.dot_general(a, b, (((1,), (1,)), ((), ())), preferred_element_type=F32, precision=prec)


def _dot_tn(a, b):
    return _dot(a.T.astype(BF16), b)


def _split_dot(m, x, terms, left=True):
    acc = None
    for _ in range(terms):
        hi = x.astype(BF16)
        part = _dot(m, hi) if left else _dot(hi, m)
        acc = part if acc is None else acc + part
        x = x - hi.astype(F32)
    return acc


def _inproj_kernel(x_ref, g_ref, w_ref, bf_ref, qb_ref, kb_ref, vb_ref, k_ref, v_ref, lf_ref, zr_ref, sg_ref):
    h = _rms(x_ref[...], g_ref[...]).astype(BF16)

    def mm(lo, n):
        return _dot(h, w_ref[:, lo:lo + n])

    qb_ref[...] = (mm(0, WIDTH) * (ATTN_SCALE * LOG2E)).astype(BF16)
    k = mm(WIDTH, WIDTH)
    k_ref[...] = k
    kb_ref[...] = k.astype(BF16)
    v = mm(2 * WIDTH, WIDTH)
    v_ref[...] = v
    vb_ref[...] = v.astype(BF16)
    lf_ref[...] = _log_sigmoid(mm(F_OFF, F_COLS) + bf_ref[...])
    for c in range(R_COLS // WIDTH):
        zr_ref[:, c * WIDTH:(c + 1) * WIDTH] = mm(R_OFF + c * WIDTH, WIDTH)
    for c in range(2 * 1024 // WIDTH):
        sg_ref[:, c * WIDTH:(c + 1) * WIDTH] = _sigmoid(mm(G_OFF + c * WIDTH, WIDTH)).astype(BF16)


def _inproj(x, g, w_packed, bf_pad, tm):
    m, d = x.shape
    ncols = w_packed.shape[1]
    row = lambda n: pl.BlockSpec((tm, n), lambda i: (i, 0))
    outs = [
        jax.ShapeDtypeStruct((m, WIDTH), BF16), jax.ShapeDtypeStruct((m, WIDTH), BF16),
        jax.ShapeDtypeStruct((m, WIDTH), BF16), jax.ShapeDtypeStruct((m, WIDTH), F32),
        jax.ShapeDtypeStruct((m, WIDTH), F32), jax.ShapeDtypeStruct((m, F_COLS), F32),
        jax.ShapeDtypeStruct((m, R_COLS), F32), jax.ShapeDtypeStruct((m, 2 * d), BF16),
    ]
    return pl.pallas_call(
        _inproj_kernel,
        out_shape=outs,
        grid=(pl.cdiv(m, tm),),
        in_specs=[row(d), _full((1, d)), _full((d, ncols)), _full((1, F_COLS))],
        out_specs=[row(WIDTH), row(WIDTH), row(WIDTH), row(WIDTH), row(WIDTH), row(F_COLS), row(R_COLS),
                   row(2 * d)],
        compiler_params=_cparams(("parallel",)),
        name="inproj",
    )(x, g, w_packed, bf_pad)


CUM_BLOCK = 256


def _cumsum_kernel(lf_ref, c_ref):
    length = lf_ref.shape[1]
    r = lax.broadcasted_iota(jnp.int32, (CUM_BLOCK, CUM_BLOCK), 0)
    c = lax.broadcasted_iota(jnp.int32, (CUM_BLOCK, CUM_BLOCK), 1)
    tri = (c <= r).astype(F32)
    carry = jnp.zeros((1, lf_ref.shape[2]), F32)
    for lo in range(0, length, CUM_BLOCK):
        n = min(CUM_BLOCK, length - lo)
        blk = _dot(tri[:n, :n], lf_ref[0, lo:lo + n, :], HIGHEST) + carry
        c_ref[0, lo:lo + n, :] = blk * LOG2E
        carry = blk[n - 1:n, :]


def _cumsum_logf(lf):
    b, length, n = lf.shape
    spec = pl.BlockSpec((1, length, n), lambda i: (i, 0, 0))
    return pl.pallas_call(
        _cumsum_kernel,
        out_shape=jax.ShapeDtypeStruct(lf.shape, F32),
        grid=(b,),
        in_specs=[spec],
        out_specs=spec,
        compiler_params=_cparams(("parallel",)),
        name="cumsum_logf",
    )(lf)


QB = 256
PAIR = 2
ATTN_GROUPS = 2


def _attn_kernel(q_ref, k_ref, v_ref, ct_ref, o_ref, *, n_full, tail):
    step_id = pl.program_id(1)
    lane_head = lax.broadcasted_iota(jnp.int32, (1, LANES), 1) // HEAD_DIM
    one = jnp.ones((), BF16)
    groups = range(ATTN_GROUPS)
    gl = lambda gi: slice(gi * LANES, (gi + 1) * LANES)

    def attend(q_rows, rows, n_off, diag_rows, diag_lo):
        qs = []
        for gi in groups:
            qt = q_ref[0, rows, gl(gi)]
            qs.append(jnp.concatenate([jnp.where(lane_head == hh, qt, jnp.zeros_like(qt)) for hh in range(PAIR)],
                                      axis=0))

        def step(kv, c0, carry, masked):
            new = []
            for gi in groups:
                kt, vt = kv[gi]
                s_all = _dot_nt(qs[gi], kt)
                for hh in range(PAIR):
                    m, acc = carry[gi * PAIR + hh]
                    head = (step_id * ATTN_GROUPS + gi) * PAIR + hh
                    s = s_all[hh * q_rows:(hh + 1) * q_rows] - ct_ref[0, head, :, pl.ds(c0, QB)]
                    if masked:
                        ri = lax.broadcasted_iota(jnp.int32, (q_rows, QB), 0)
                        ci = lax.broadcasted_iota(jnp.int32, (q_rows, QB), 1)
                        s = jnp.where(ci <= ri, s, NEG_INF)
                    m_new = jnp.maximum(m, jnp.max(s, axis=-1, keepdims=True))
                    p = jnp.exp2(s - m_new).astype(BF16)
                    acc = jnp.exp2(m - m_new) * acc + _dot(p, jnp.where(lane_head == hh, vt, one))
                    new.append((m_new, acc))
            return tuple(new)

        def body(j, carry):
            c0 = pl.multiple_of(j * QB, QB)
            kv = [(k_ref[0, pl.ds(c0, QB), gl(gi)], v_ref[0, pl.ds(c0, QB), gl(gi)]) for gi in groups]
            return step(kv, c0, carry, False)

        init = tuple((jnp.full((q_rows, 1), NEG_INF, F32), jnp.zeros((q_rows, LANES), F32))
                     for _ in range(ATTN_GROUPS * PAIR))
        carry = step(diag_rows, diag_lo, lax.fori_loop(0, n_off, body, init), True)
        for gi in groups:
            outs = [acc / pltpu.roll(acc, HEAD_DIM, axis=1) for _, acc in carry[gi * PAIR:(gi + 1) * PAIR]]
            o_ref[0, rows, gl(gi)] = jnp.where(lane_head == 0, outs[0], outs[1]).astype(o_ref.dtype)

    def qblock(i, _):
        r0 = pl.multiple_of(i * QB, QB)
        rows = pl.ds(r0, QB)
        attend(QB, rows, i, [(k_ref[0, rows, gl(gi)], v_ref[0, rows, gl(gi)]) for gi in groups], r0)
        return 0

    lax.fori_loop(0, n_full, qblock, 0)
    if tail:
        lo = n_full * QB
        pad = jnp.zeros((QB - tail, LANES), k_ref.dtype)
        padded = lambda ref, gi: jnp.concatenate([ref[0, lo:lo + tail, gl(gi)], pad], axis=0)
        attend(tail, slice(lo, lo + tail), n_full, [(padded(k_ref, gi), padded(v_ref, gi)) for gi in groups], lo)


def _fox_prompt(qb, kb, vb, c_t):
    b, length, _ = qb.shape
    n_full, tail = divmod(length, QB)
    assert tail % 16 == 0 and c_t.shape[3] == (n_full + (1 if tail else 0)) * QB
    qkv = pl.BlockSpec((1, length, LANES * ATTN_GROUPS), lambda i, g: (i, 0, g))
    return pl.pallas_call(
        functools.partial(_attn_kernel, n_full=n_full, tail=tail),
        out_shape=jax.ShapeDtypeStruct(qb.shape, BF16),
        grid=(b, HEADS // (PAIR * ATTN_GROUPS)),
        in_specs=[qkv, qkv, qkv, pl.BlockSpec((1, HEADS, 1, c_t.shape[3]), lambda i, g: (i, 0, 0, 0))],
        out_specs=qkv,
        compiler_params=_cparams(("parallel", "arbitrary")),
        name="fox_prompt",
    )(qb, kb, vb, c_t)


def _head_sum_matrix():
    r = lax.broadcasted_iota(jnp.int32, (WIDTH, WIDTH), 0) // HEAD_DIM
    c = lax.broadcasted_iota(jnp.int32, (WIDTH, WIDTH), 1) // HEAD_DIM
    return (r == c).astype(BF16)


def _rwkv_pointwise(zs, p):
    (w0, w2, a0, a2, g2, k_k, k_a, r_k) = p
    r = zs[:, 0:WIDTH]
    k = zs[:, WIDTH:2 * WIDTH]
    v = zs[:, 2 * WIDTH:3 * WIDTH]
    zw = zs[:, ZW_OFF:ZW_OFF + LANES]
    za = zs[:, ZA_OFF:ZA_OFF + LANES]
    zg = zs[:, ZG_OFF:ZG_OFF + ZG_COLS]
    w_log = _log_sigmoid(w0 + _dot(jnp.tanh(zw).astype(BF16), w2)) - 0.5
    logdec = -jnp.exp(w_log)
    a = _sigmoid(a0 + _dot(za.astype(BF16), a2))
    g = _dot(_sigmoid(zg).astype(BF16), g2)
    hs = _head_sum_matrix()
    kk = k * k_k
    norm = jnp.sqrt(_split_dot(hs, kk * kk, 2, left=False))
    kk = kk / jnp.maximum(norm, 1e-12)
    km = k * (1.0 + (a - 1.0) * k_a)
    bonus = _split_dot(hs, r * km * r_k, 2, left=False) * v
    return r, km, v, kk, a, logdec, g, bonus


def _group_norm_out(y, ln_w, ln_b, bonus, g):
    mean = jnp.mean(y, axis=-1, keepdims=True)
    var = jnp.mean(jnp.square(y - mean), axis=-1, keepdims=True)
    yn = (y - mean) * lax.rsqrt(var + GN_EPS) * ln_w + ln_b
    return (yn + bonus) * g


RCHUNK = 128


def _rwkv_chunk_kernel(zr_ref, zp_ref, sh_ref, mu_ref, w0_ref, w2_ref, a0_ref, a2_ref, g2_ref, kk_ref, ka_ref,
                       rk_ref, lnw_ref, lnb_ref, out_ref, wkv_ref, s_ref, *, length):
    ci = pl.program_id(1)
    n = RCHUNK

    @pl.when(ci == 0)
    def _():
        s_ref[...] = jnp.zeros_like(s_ref)

    zr = zr_ref[0]
    row = lax.broadcasted_iota(jnp.int32, (n, 1), 0)
    prev = jnp.where(ci == 0, sh_ref[0], zp_ref[0, 7:8, :])
    zprev = jnp.where(row == 0, prev, pltpu.roll(zr, 1, axis=0))
    valid = (ci * n + row) < length
    zs = jnp.where(valid, zr + (zprev - zr) * mu_ref[...], 0.0)
    params = (w0_ref[...], w2_ref[...], a0_ref[...], a2_ref[...], g2_ref[...], kk_ref[...], ka_ref[...],
              rk_ref[...])
    r, km, v, kk, a, logdec, g, bonus = _rwkv_pointwise(zs, params)
    logdec = jnp.where(valid, logdec, 0.0)

    ri = lax.broadcasted_iota(jnp.int32, (n, n), 0)
    cj = lax.broadcasted_iota(jnp.int32, (n, n), 1)
    incl = cj <= ri
    strict = cj < ri
    eye = (ri == cj).astype(F32)
    eye_h = eye[:HEAD_DIM, :HEAD_DIM]
    logp = _split_dot(incl.astype(BF16), logdec, 3)
    logp_c = logp[n - 1:n, :]
    e_pos = jnp.exp(logp)
    e_neg = jnp.exp(-logp)
    e_end = jnp.exp(logp_c - logp)
    rt_all = r * e_pos
    pc_all = jnp.exp(logp_c)
    heads = range(HEADS)
    sls = [slice(h * HEAD_DIM, (h + 1) * HEAD_DIM) for h in heads]
    cut = lambda x: [x[:, sl].astype(BF16) for sl in sls]
    rt, kt, at, bt = cut(rt_all), cut(km * e_neg), cut(-kk * jnp.exp(logp - logdec)), cut(kk * a * e_neg)
    bh, kh, vh = cut(kk * a * e_end), cut(km * e_end), cut(v)

    m_all = [_dot_nt(jnp.concatenate([at[h], rt[h]], axis=0), jnp.concatenate([bt[h], kt[h]], axis=0)) for h in heads]
    a_ab = [jnp.where(strict, m[:n, :n], 0.0) for m in m_all]
    a_ak = [jnp.where(strict, m[:n, n:], 0.0).astype(BF16) for m in m_all]
    m_rb = [jnp.where(incl, m[n:, :n], 0.0).astype(BF16) for m in m_all]
    m_rk = [jnp.where(incl, m[n:, n:], 0.0).astype(BF16) for m in m_all]
    t_inv = [eye for _ in heads]
    pw = a_ab
    span = 1
    while span < n:
        pt = [_dot(pw[h].astype(BF16), jnp.concatenate([pw[h], t_inv[h]], axis=1).astype(BF16)) for h in heads]
        pw = [x[:, :n] for x in pt]
        t_inv = [t_inv[h] + pt[h][:, n:] for h in heads]
        span *= 2
    t_bf = [x.astype(BF16) for x in t_inv]
    w_m = [_dot(t_bf[h], at[h]) for h in heads]
    akv = [_dot(a_ak[h], vh[h]).astype(BF16) for h in heads]
    u0 = [_dot(t_bf[h], akv[h]) for h in heads]
    w_bf = [x.astype(BF16) for x in w_m]
    u_bf = [x.astype(BF16) for x in u0]
    qp = [rt_all[:, sls[h]] + _dot(m_rb[h], w_bf[h]) for h in heads]
    y0 = [_dot(m_rb[h], u_bf[h]) + _dot(m_rk[h], vh[h]) for h in heads]
    gt = [eye_h * pc_all[:, sls[h]] + _dot_tn(w_m[h], bh[h]) for h in heads]
    et = [_dot_tn(u0[h], bh[h]) + _dot_tn(v[:, sls[h]], kh[h]) for h in heads]
    for h in heads:
        sl = sls[h]
        s_h = s_ref[h].astype(BF16)
        y = _dot_nt(qp[h].astype(BF16), s_h) + y0[h]
        s_ref[h] = _dot(s_h, gt[h].astype(BF16)) + et[h]
        out_ref[0, :, sl] = _group_norm_out(y, lnw_ref[:, sl], lnb_ref[:, sl], bonus[:, sl],
                                            g[:, sl]).astype(out_ref.dtype)

    @pl.when(ci == pl.num_programs(1) - 1)
    def _():
        wkv_ref[0] = s_ref[...]


def _rwkv_prompt(zr, shift0, rp):
    b, length, _ = zr.shape
    n = RCHUNK
    nchunk = pl.cdiv(length, n)
    vec = _full((1, WIDTH))
    return pl.pallas_call(
        functools.partial(_rwkv_chunk_kernel, length=length),
        out_shape=[jax.ShapeDtypeStruct((b, length, WIDTH), BF16),
                   jax.ShapeDtypeStruct((b, HEADS, HEAD_DIM, HEAD_DIM), F32)],
        grid=(b, nchunk),
        in_specs=[pl.BlockSpec((1, n, R_COLS), lambda i, c: (i, c, 0)),
                  pl.BlockSpec((1, 8, R_COLS), lambda i, c: (i, jnp.maximum(c * (n // 8) - 1, 0), 0)),
                  pl.BlockSpec((1, 1, R_COLS), lambda i, c: (i, 0, 0)),
                  _full((1, R_COLS)), vec, _full((LANES, WIDTH)), vec, _full((LANES, WIDTH)),
                  _full((ZG_COLS, WIDTH)), vec, vec, vec, vec, vec],
        out_specs=[pl.BlockSpec((1, n, WIDTH), lambda i, c: (i, c, 0)),
                   pl.BlockSpec((1, HEADS, HEAD_DIM, HEAD_DIM), lambda i, c: (i, 0, 0, 0))],
        scratch_shapes=[pltpu.VMEM((HEADS, HEAD_DIM, HEAD_DIM), F32)],
        compiler_params=_cparams(("parallel", "arbitrary")),
        name="rwkv_prompt",
    )(zr, zr, shift0, rp["mu"], rp["w0"], rp["w2"], rp["a0"], rp["a2"], rp["g2"], rp["k_k"], rp["k_a"],
      rp["r_k"], rp["ln_w"], rp["ln_b"])


DEC_ROWS = 8


def _rwkv_decode_kernel(zr_ref, sh_ref, st_ref, mu_ref, w0_ref, w2_ref, a0_ref, a2_ref, g2_ref, kk_ref, ka_ref,
                        rk_ref, lnw_ref, lnb_ref, out_ref, ns_ref, vec_ref):
    zr = zr_ref[...]
    zs = zr + (sh_ref[...] - zr) * mu_ref[...]
    params = (w0_ref[...], w2_ref[...], a0_ref[...], a2_ref[...], g2_ref[...], kk_ref[...], ka_ref[...],
              rk_ref[...])
    r, km, v, kk, a, logdec, g, bonus = _rwkv_pointwise(zs, params)
    for idx, x in enumerate((r, km, v, kk, kk * a, jnp.exp(logdec), g, bonus)):
        for s in range(DEC_ROWS):
            vec_ref[idx, s] = x[s:s + 1, :]
    ri = lax.broadcasted_iota(jnp.int32, (HEAD_DIM, HEAD_DIM), 0)
    cj = lax.broadcasted_iota(jnp.int32, (HEAD_DIM, HEAD_DIM), 1)
    eye = ri == cj

    def per_row(s, _):
        for h in range(HEADS):
            sl = slice(h * HEAD_DIM, (h + 1) * HEAD_DIM)
            r_h, k_h, v_h, kk_h, ka_h, w_h, g_h, bo_h = (vec_ref[i, s, :, sl] for i in range(8))
            st = st_ref[s, h]
            sa = jnp.sum(st * (-kk_h), axis=-1, keepdims=True)
            v_col = jnp.sum(jnp.where(eye, v_h, 0.0), axis=-1, keepdims=True)
            st = st * w_h + sa * ka_h + v_col * k_h
            ns_ref[s, h] = st
            y_col = jnp.sum(st * r_h, axis=-1, keepdims=True)
            y = jnp.sum(jnp.where(eye, y_col, 0.0), axis=0, keepdims=True)
            out_ref[s, :, sl] = _group_norm_out(y, lnw_ref[:, sl], lnb_ref[:, sl], bo_h, g_h)
        return 0

    lax.fori_loop(0, DEC_ROWS, per_row, 0)


def _rwkv_decode(zr, shift, state, rp):
    n = zr.shape[0]
    vec = _full((1, WIDTH))
    rows = lambda c: pl.BlockSpec((DEC_ROWS, c), lambda i: (i, 0))
    st = pl.BlockSpec((DEC_ROWS, HEADS, HEAD_DIM, HEAD_DIM), lambda i: (i, 0, 0, 0))
    return pl.pallas_call(
        _rwkv_decode_kernel,
        out_shape=[jax.ShapeDtypeStruct((n, 1, WIDTH), F32), jax.ShapeDtypeStruct(state.shape, F32)],
        grid=(n // DEC_ROWS,),
        in_specs=[rows(R_COLS), rows(R_COLS), st, _full((1, R_COLS)), vec, _full((LANES, WIDTH)), vec,
                  _full((LANES, WIDTH)), _full((ZG_COLS, WIDTH)), vec, vec, vec, vec, vec],
        out_specs=[pl.BlockSpec((DEC_ROWS, 1, WIDTH), lambda i: (i, 0, 0)), st],
        scratch_shapes=[pltpu.VMEM((8, DEC_ROWS, 1, WIDTH), F32)],
        compiler_params=_cparams(("parallel",)),
        name="rwkv_decode",
    )(zr, shift, state, rp["mu"], rp["w0"], rp["w2"], rp["a0"], rp["a2"], rp["g2"], rp["k_k"], rp["k_a"],
      rp["r_k"], rp["ln_w"], rp["ln_b"])


SUFFIX_PAGES = 512
DEC_PAGES = 8


def _page_suffix_kernel(lf_ref, e_ref, tot_ref):
    pages, heads, plen = lf_ref.shape
    x = lf_ref[...].reshape(pages * heads, plen)
    u = lax.broadcasted_iota(jnp.int32, (plen, plen), 0)
    t = lax.broadcasted_iota(jnp.int32, (plen, plen), 1)
    after = _split_dot((u > t).astype(BF16), x, 3, left=False)
    total = _split_dot(jnp.ones((plen, plen), BF16), x, 3, left=False)
    e_ref[...] = (after * LOG2E).reshape(pages, heads, plen)
    tot_ref[...] = (total * LOG2E).reshape(pages, heads, plen)


def _page_suffix(lft):
    n_pool, heads, plen = lft.shape
    spec = pl.BlockSpec((SUFFIX_PAGES, heads, plen), lambda i: (i, 0, 0))
    return pl.pallas_call(
        _page_suffix_kernel,
        out_shape=[jax.ShapeDtypeStruct(lft.shape, F32)] * 2,
        grid=(pl.cdiv(n_pool, SUFFIX_PAGES),),
        in_specs=[spec],
        out_specs=[spec, spec],
        compiler_params=_cparams(("parallel",)),
        name="page_suffix",
    )(lft)


def _fox_decode_kernel(pt_ref, q_ref, kn_ref, vn_ref, lfn_ref, *refs):
    del pt_ref
    k_refs, v_refs, e_refs, t_refs = (refs[i * DEC_PAGES:(i + 1) * DEC_PAGES] for i in range(4))
    o_ref, m_ref, l_ref, acc_ref, carry_ref = refs[4 * DEC_PAGES:]
    g = pl.program_id(1)
    plen = k_refs[0].shape[4]

    @pl.when(g == 0)
    def _():
        m_ref[...] = jnp.full_like(m_ref, NEG_INF)
        l_ref[...] = jnp.zeros_like(l_ref)
        acc_ref[...] = jnp.zeros_like(acc_ref)
        carry_ref[...] = lfn_ref[0] * LOG2E

    for h in range(HEADS):
        row = slice(h, h + 1)
        q_b = jnp.broadcast_to(q_ref[0, h], (HEAD_DIM, plen))
        carry = carry_ref[row, :]
        scores = []
        for i in range(DEC_PAGES):
            s = jnp.sum(k_refs[i][0, 0, h] * q_b, axis=0, keepdims=True)
            scores.append(s + e_refs[i][0, row, :] + carry)
            carry = carry + t_refs[i][0, row, 0:1]
        carry_ref[row, :] = carry
        top = scores[0]
        for s in scores[1:]:
            top = jnp.maximum(top, s)
        m_old = m_ref[row, :]
        m_new = jnp.maximum(m_old, jnp.max(top, axis=-1, keepdims=True))
        alpha = jnp.exp2(m_old - m_new)
        m_ref[row, :] = m_new
        psum = None
        pv = None
        for i in range(DEC_PAGES):
            pr = jnp.exp2(scores[i] - m_new)
            part = v_refs[i][0, 0, h] * pr
            psum = pr if psum is None else psum + pr
            pv = part if pv is None else pv + part
        l_ref[row, :] = alpha * l_ref[row, :] + jnp.sum(psum, axis=-1, keepdims=True)
        acc_ref[h] = alpha * acc_ref[h] + pv

    @pl.when(g == pl.num_programs(1) - 1)
    def _():
        for h in range(HEADS):
            row = slice(h, h + 1)
            s_new = jnp.sum(q_ref[0, h] * kn_ref[0, h], axis=0, keepdims=True)
            m_fin = jnp.maximum(m_ref[row, :], s_new)
            a_fin = jnp.exp2(m_ref[row, :] - m_fin)
            p_new = jnp.exp2(s_new - m_fin)
            num = a_fin * jnp.sum(acc_ref[h], axis=-1, keepdims=True) + p_new * vn_ref[0, h]
            o_ref[0, h] = num / (a_fin * l_ref[row, :] + p_new)


def _fox_decode(page_table, q, k_new, v_new, lf_new, cache_kt, cache_vt, e_page, tot):
    n, n_pages = page_table.shape
    page = cache_kt.shape[4]
    assert n_pages % DEC_PAGES == 0
    tok = pl.BlockSpec((1, HEADS, HEAD_DIM, 1), lambda s, g, pt: (s, 0, 0, 0))

    def paged(shape, i, lead):
        idx = lambda s, g, pt: lead + (pt[s, n_pages - 1 - (g * DEC_PAGES + i)],) + (0,) * (len(shape) - len(lead) - 1)
        return pl.BlockSpec(shape, idx)

    slots = range(DEC_PAGES)
    kv_shape = (1, 1, HEADS, HEAD_DIM, page)
    grid_spec = pltpu.PrefetchScalarGridSpec(
        num_scalar_prefetch=1,
        grid=(n, n_pages // DEC_PAGES),
        in_specs=[tok, tok, tok, pl.BlockSpec((1, HEADS, 1), lambda s, g, pt: (s, 0, 0))]
        + [paged(kv_shape, i, (0,)) for i in slots] + [paged(kv_shape, i, (0,)) for i in slots]
        + [paged((1, HEADS, page), i, ()) for i in slots] + [paged((1, HEADS, page), i, ()) for i in slots],
        out_specs=tok,
        scratch_shapes=[pltpu.VMEM((HEADS, 1), F32), pltpu.VMEM((HEADS, 1), F32),
                        pltpu.VMEM((HEADS, HEAD_DIM, page), F32), pltpu.VMEM((HEADS, 1), F32)],
    )
    return pl.pallas_call(
        _fox_decode_kernel,
        out_shape=jax.ShapeDtypeStruct((n, HEADS, HEAD_DIM, 1), F32),
        grid_spec=grid_spec,
        compiler_params=_cparams(("parallel", "arbitrary")),
        name="fox_decode",
    )(page_table, q, k_new, v_new, lf_new, *([cache_kt] * DEC_PAGES), *([cache_vt] * DEC_PAGES),
      *([e_page] * DEC_PAGES), *([tot] * DEC_PAGES))


def _outproj_kernel(oa_ref, orw_ref, sg_ref, x_ref, woa_ref, wor_ref, wout_ref, nf_ref, wrt_ref, brc_ref, x1_ref,
                    h2t_ref, gate_ref, gate3_ref, *, n_valid):
    i = pl.program_id(0)
    d = x_ref.shape[1]
    n_exp = wrt_ref.shape[0]

    @pl.when(i < n_valid)
    def _():
        a = _dot(oa_ref[...], woa_ref[...])
        b = _dot(orw_ref[...], wor_ref[...])
        merged = sg_ref[:, :d].astype(F32) * a + sg_ref[:, d:].astype(F32) * b
        x1 = x_ref[...] + _dot(merged.astype(BF16), wout_ref[...])
        x1_ref[...] = x1
        h2 = _rms(x1, nf_ref[...])
        h2t_ref[...] = h2.T.astype(BF16)
        work = _dot_nt(wrt_ref[...], h2, HIGHEST) + brc_ref[...]
        row = lax.broadcasted_iota(jnp.int32, work.shape, 0)
        top = None
        denom = jnp.zeros((1, work.shape[1]), F32)
        picks = []
        for _ in range(TOP_K):
            mx = jnp.max(work, axis=0, keepdims=True)
            idx = jnp.min(jnp.where(work == mx, row, n_exp), axis=0, keepdims=True)
            hit = row == idx
            top = mx if top is None else top
            e = jnp.exp(mx - top)
            denom = denom + e
            picks.append((hit, e))
            work = jnp.where(hit, -jnp.inf, work)
        gate = jnp.zeros(work.shape, F32)
        for hit, e in picks:
            gate = jnp.where(hit, e / denom, gate)
        gate_ref[...] = gate
        for ex in range(n_exp):
            gate3_ref[ex] = gate[ex:ex + 1, :]

    @pl.when(i >= n_valid)
    def _():
        x1_ref[...] = jnp.zeros_like(x1_ref)
        h2t_ref[...] = jnp.zeros_like(h2t_ref)
        gate_ref[...] = jnp.zeros_like(gate_ref)
        gate3_ref[...] = jnp.zeros_like(gate3_ref)


def _outproj(o_attn, o_rwkv, sg, x, w_oa, w_or, w_out, norm_ffn, w_router_t, b_router_col, tm, m_pad):
    m, d = x.shape
    n_exp = w_router_t.shape[0]
    n_valid = m // tm
    assert m % tm == 0 and m_pad % tm == 0
    last = n_valid - 1
    row = lambda n: pl.BlockSpec((tm, n), lambda i: (jnp.minimum(i, last), 0))
    return pl.pallas_call(
        functools.partial(_outproj_kernel, n_valid=n_valid),
        out_shape=[jax.ShapeDtypeStruct((m_pad, d), F32), jax.ShapeDtypeStruct((d, m_pad), BF16),
                   jax.ShapeDtypeStruct((n_exp, m_pad), F32), jax.ShapeDtypeStruct((n_exp, 1, m_pad), F32)],
        grid=(m_pad // tm,),
        in_specs=[row(WIDTH), row(WIDTH), row(2 * d), row(d), _full((WIDTH, d)), _full((WIDTH, d)), _full((d, d)),
                  _full((1, d)), _full((n_exp, d)), _full((n_exp, 1))],
        out_specs=[pl.BlockSpec((tm, d), lambda i: (i, 0)), pl.BlockSpec((d, tm), lambda i: (0, i)),
                   pl.BlockSpec((n_exp, tm), lambda i: (0, i)), pl.BlockSpec((n_exp, 1, tm), lambda i: (0, 0, i))],
        compiler_params=_cparams(("parallel",)),
        name="outproj_router",
    )(o_attn, o_rwkv, sg, x, w_oa, w_or, w_out, norm_ffn, w_router_t, b_router_col)


MOE_TOKENS = 1792
MOE_SLOTS = 256
MOE_FF_CHUNK = 256


def _moe_kernel(h2t_ref, gate_ref, gate3_ref, wu_ref, bu_ref, wd_ref, bd_ref, o_ref, acc_ref, rank_ref, xg_ref,
                sg_ref, dn_ref):
    e = pl.program_id(1)
    last = pl.num_programs(1) - 1
    d, tt = h2t_ref.shape
    n_exp = gate_ref.shape[0]
    d_ff = wd_ref.shape[2]
    reps = MOE_SLOTS // LANES
    cols = lambda ref, lo, n: jnp.concatenate([ref[0, lo:lo + n, :]] * reps, axis=1)

    def gather(ex, blk):
        g_row = gate3_ref[ex]
        slot_of = jnp.where(g_row > 0.0, rank_ref[ex], -1.0)
        slot = (blk * MOE_SLOTS + lax.broadcasted_iota(jnp.int32, (MOE_SLOTS, 1), 0)).astype(F32)
        sel = slot_of == slot
        sel_one = jnp.where(sel, 1.0, 0.0).astype(BF16)
        return _dot_nt(h2t_ref[...], sel_one).astype(BF16), jnp.where(sel, g_row, 0.0).astype(BF16)

    def ffn(xg):
        dn = None
        for lo in range(0, d_ff, MOE_FF_CHUNK):
            x_glu = _dot(wu_ref[0, lo:lo + MOE_FF_CHUNK, :], xg) + cols(bu_ref, lo, MOE_FF_CHUNK)
            x_lin = _dot(wu_ref[0, d_ff + lo:d_ff + lo + MOE_FF_CHUNK, :], xg) + cols(bu_ref, d_ff + lo, MOE_FF_CHUNK)
            x_glu = jnp.minimum(x_glu, SWIGLU_LIMIT)
            x_lin = jnp.clip(x_lin, -SWIGLU_LIMIT, SWIGLU_LIMIT)
            act = (x_glu * _sigmoid(SWIGLU_ALPHA * x_glu) * (x_lin + 1.0)).astype(BF16)
            part = _dot(wd_ref[0, :, lo:lo + MOE_FF_CHUNK], act)
            dn = part if dn is None else dn + part
        return jnp.concatenate([(dn[lo:lo + MOE_FF_CHUNK] + cols(bd_ref, lo, MOE_FF_CHUNK)).astype(BF16)
                                for lo in range(0, d, MOE_FF_CHUNK)], axis=0)

    def scatter(dn, sel_gate):
        for lo in range(0, d, MOE_FF_CHUNK):
            rows = slice(lo, lo + MOE_FF_CHUNK)
            acc_ref[rows, :] += _dot(dn[rows], sel_gate)

    @pl.when(e == 0)
    def _():
        acc_ref[...] = jnp.zeros_like(acc_ref)
        tr = lax.broadcasted_iota(jnp.int32, (tt, tt), 0)
        tc = lax.broadcasted_iota(jnp.int32, (tt, tt), 1)
        hit_all = jnp.where(gate_ref[...] > 0.0, 1.0, 0.0).astype(BF16)
        rank = _dot(hit_all, (tr < tc).astype(BF16))
        for ex in range(n_exp):
            rank_ref[ex] = rank[ex:ex + 1, :]
        xg0, sg0 = gather(0, 0)
        xg_ref[0] = xg0
        sg_ref[0] = sg0
        dn_ref[...] = jnp.zeros_like(dn_ref)
        sg_ref[2] = jnp.zeros_like(sg0)

    scatter(dn_ref[...], sg_ref[(e + 2) % 3])
    dn_new = ffn(xg_ref[e % 2])
    xg_next, sg_next = gather(jnp.minimum(e + 1, last), 0)
    xg_ref[(e + 1) % 2] = xg_next
    sg_ref[(e + 1) % 3] = sg_next
    dn_ref[...] = dn_new

    def extra_pass(blk, _):
        xg, sel_gate = gather(e, blk)
        scatter(ffn(xg), sel_gate)
        return 0

    count = jnp.sum(jnp.where(gate3_ref[e] > 0.0, 1.0, 0.0)).astype(jnp.int32)
    lax.fori_loop(1, (count + MOE_SLOTS - 1) // MOE_SLOTS, extra_pass, 0)

    @pl.when(e == last)
    def _():
        scatter(dn_ref[...], sg_ref[last % 3])
        o_ref[...] = acc_ref[...].T.astype(o_ref.dtype)


def _moe(h2t, gate, gate3, w_up_t, b_up, w_down_t, b_down, tt):
    d, m_pad = h2t.shape
    n_exp, d_up, _ = w_up_t.shape
    assert m_pad % tt == 0
    return pl.pallas_call(
        _moe_kernel,
        out_shape=jax.ShapeDtypeStruct((m_pad, d), BF16),
        grid=(m_pad // tt, n_exp),
        in_specs=[pl.BlockSpec((d, tt), lambda i, e: (0, i)),
                  pl.BlockSpec((n_exp, tt), lambda i, e: (0, i)),
                  pl.BlockSpec((n_exp, 1, tt), lambda i, e: (0, 0, i)),
                  pl.BlockSpec((1, d_up, d), lambda i, e: (e, 0, 0)),
                  pl.BlockSpec((1, d_up, LANES), lambda i, e: (e, 0, 0)),
                  pl.BlockSpec((1, d, w_down_t.shape[2]), lambda i, e: (e, 0, 0)),
                  pl.BlockSpec((1, d, LANES), lambda i, e: (e, 0, 0))],
        out_specs=pl.BlockSpec((tt, d), lambda i, e: (i, 0)),
        scratch_shapes=[pltpu.VMEM((d, tt), F32), pltpu.VMEM((n_exp, 1, tt), F32),
                        pltpu.VMEM((2, d, MOE_SLOTS), BF16), pltpu.VMEM((3, MOE_SLOTS, tt), BF16),
                        pltpu.VMEM((d, MOE_SLOTS), BF16)],
        compiler_params=_cparams(("parallel", "arbitrary")),
        name="moe_routed",
    )(h2t, gate, gate3, w_up_t, b_up, w_down_t, b_down)


def _final_kernel(x1_ref, moe_ref, g_ref, y_ref):
    y_ref[...] = _rms(x1_ref[...] + moe_ref[...].astype(F32), g_ref[...])


def _final_norm(x1, moe, g, m, tm):
    d = x1.shape[1]
    row = pl.BlockSpec((tm, d), lambda i: (i, 0))
    return pl.pallas_call(
        _final_kernel,
        out_shape=jax.ShapeDtypeStruct((m, d), F32),
        grid=(pl.cdiv(m, tm),),
        in_specs=[row, row, _full((1, d))],
        out_specs=row,
        compiler_params=_cparams(("parallel",)),
        name="final_norm",
    )(x1, moe, g)


def _pad_cols(a, n):
    return jnp.pad(a, ((0, 0), (0, n - a.shape[1])))


def _pack_rwkv_cols(a):
    o = 3 * WIDTH
    return jnp.concatenate([
        a[:, :o], _pad_cols(a[:, o:o + DECAY_LORA], LANES),
        _pad_cols(a[:, o + DECAY_LORA:o + DECAY_LORA + AAA_LORA], LANES),
        _pad_cols(a[:, o + DECAY_LORA + AAA_LORA:], ZG_COLS)], axis=1)


def _unpack_rwkv_cols(a):
    return jnp.concatenate([a[..., :ZW_OFF], a[..., ZW_OFF:ZW_OFF + DECAY_LORA], a[..., ZA_OFF:ZA_OFF + AAA_LORA],
                            a[..., ZG_OFF:ZG_OFF + GATE_LORA]], axis=-1)


def _pad_rows(a, n):
    return jnp.pad(a, ((0, n - a.shape[0]), (0, 0)))


def kernel(x_prompt, x_sample, cache_k, cache_v, cache_logf, page_table, state_wkv, state_shift, meta_tokens, norm_mix, w_in, b_f, w_oa, rwkv_mu, rwkv_w0, rwkv_w2, rwkv_a0, rwkv_a2, rwkv_g2, rwkv_k_k, rwkv_k_a, rwkv_r_k, rwkv_ln_w, rwkv_ln_b, w_or, w_out, norm_ffn, w_router, b_router, w_up, b_up, w_down, b_down, norm_final):
    b, seq, d = x_prompt.shape
    db = x_sample.shape[0]
    depth = w_in.shape[0]
    assert depth == 1 and x_sample.shape[1] == 1
    length = N_META + seq
    n_exp = w_router.shape[2]
    a_cols = 3 * WIDTH + HEADS
    r_cols = 3 * WIDTH + DECAY_LORA + AAA_LORA + GATE_LORA

    wi = w_in[0]
    w_packed = jnp.concatenate([
        wi[:, :QKV_COLS], _pad_cols(wi[:, QKV_COLS:a_cols], F_COLS),
        _pack_rwkv_cols(wi[:, a_cols:a_cols + r_cols]), wi[:, a_cols + r_cols:]], axis=1).astype(BF16)
    bf_pad = _pad_cols(b_f[0][None, :], F_COLS)
    rp = {
        "mu": _pack_rwkv_cols(rwkv_mu[0][None, :]),
        "w0": rwkv_w0[0][None, :], "a0": rwkv_a0[0][None, :],
        "w2": _pad_rows(rwkv_w2[0], LANES).astype(BF16), "a2": _pad_rows(rwkv_a2[0], LANES).astype(BF16),
        "g2": _pad_rows(rwkv_g2[0], ZG_COLS).astype(BF16),
        "k_k": rwkv_k_k[0][None, :], "k_a": rwkv_k_a[0][None, :], "r_k": rwkv_r_k[0].reshape(1, WIDTH),
        "ln_w": rwkv_ln_w[0][None, :], "ln_b": rwkv_ln_b[0][None, :],
    }
    g_mix = norm_mix[0][None, :]
    g_ffn = norm_ffn[0][None, :]
    g_fin = norm_final[None, :]
    woa, wor, wout = w_oa[0].astype(BF16), w_or[0].astype(BF16), w_out[0].astype(BF16)
    wr_t = jnp.transpose(w_router[0])
    br_col = b_router[0][:, None]
    wu_t = jnp.transpose(w_up[0], (0, 2, 1)).astype(BF16)
    wd_t = jnp.transpose(w_down[0], (0, 2, 1)).astype(BF16)
    bu = jnp.broadcast_to(b_up[0][:, :, None], b_up.shape[1:] + (LANES,))
    bd = jnp.broadcast_to(b_down[0][:, :, None], b_down.shape[1:] + (LANES,))

    meta = jnp.broadcast_to(meta_tokens.astype(x_prompt.dtype)[None], (b, N_META, d))
    m_p = b * length
    xp = jnp.concatenate([meta, x_prompt], axis=1).reshape(m_p, d)
    qb, kb, vb, k_p, v_p, lf_p, zr_p, sg_p = _inproj(xp, g_mix, w_packed, bf_pad, 512)
    c2 = _cumsum_logf(lf_p.reshape(b, length, F_COLS))
    lpad = pl.cdiv(length, QB) * QB
    c_t = jnp.pad(jnp.transpose(c2[:, :, :HEADS], (0, 2, 1)), ((0, 0), (0, 0), (0, lpad - length)))[:, :, None, :]
    as3 = lambda a: a.reshape(b, length, a.shape[-1])
    o_attn = _fox_prompt(as3(qb), as3(kb), as3(vb), c_t)
    zr3 = as3(zr_p)
    o_rwkv, wkv_p = _rwkv_prompt(zr3, jnp.zeros((b, 1, R_COLS), F32), rp)
    m_pad = pl.cdiv(m_p, MOE_TOKENS) * MOE_TOKENS
    x1_p, h2t_p, gate_p, gate3_p = _outproj(o_attn.reshape(m_p, WIDTH), o_rwkv.reshape(m_p, WIDTH), sg_p, xp, woa,
                                            wor, wout, g_ffn, wr_t, br_col, 256, m_pad)
    moe_p = _moe(h2t_p, gate_p, gate3_p, wu_t, bu, wd_t, bd, MOE_TOKENS)
    y_p = _final_norm(x1_p, moe_p, g_fin, m_p, 512)

    xs = x_sample.reshape(db, d)
    qs, _, _, k_s, v_s, lf_s, zr_s, sg_s = _inproj(xs, g_mix, w_packed, bf_pad, db)
    col = lambda a: a.reshape(db, HEADS, HEAD_DIM, 1)
    to_lanes = lambda c: jnp.transpose(c, (0, 1, 3, 4, 2))
    e_page, tot_page = _page_suffix(jnp.transpose(cache_logf[0], (0, 2, 1)))
    o_attn_s = _fox_decode(page_table, col(qs.astype(F32)), col(k_s), col(v_s), lf_s[:, :HEADS, None],
                           to_lanes(cache_k), to_lanes(cache_v), e_page, tot_page)
    o_rwkv_s, wkv_s = _rwkv_decode(zr_s, _pack_rwkv_cols(state_shift[0, :, 0, :]), state_wkv[0], rp)
    x1_s, h2t_s, gate_s, gate3_s = _outproj(o_attn_s.reshape(db, WIDTH).astype(BF16),
                                            o_rwkv_s.reshape(db, WIDTH).astype(BF16), sg_s, xs, woa, wor, wout, g_ffn,
                                            wr_t, br_col, db, db)
    moe_s = _moe(h2t_s, gate_s, gate3_s, wu_t, bu, wd_t, bd, db)
    y_s = _final_norm(x1_s, moe_s, g_fin, db, db)

    y_prompt = y_p.reshape(b, length, d)[:, N_META:]
    y_sample = y_s.reshape(db, 1, d)
    hd = lambda a, n, t: a.reshape(1, n, t, HEADS, HEAD_DIM)
    logf_prompt = lf_p[:, :HEADS].reshape(1, b, length, HEADS)
    logf_sample = lf_s[:, :HEADS].reshape(1, db, 1, HEADS)
    shift_prompt = _unpack_rwkv_cols(zr3[:, length - 1:length, :])[None]
    shift_sample = _unpack_rwkv_cols(zr_s)[None, :, None, :]
    return (y_prompt, y_sample, hd(k_p, b, length), hd(v_p, b, length), logf_prompt, hd(k_s, db, 1),
            hd(v_s, db, 1), logf_sample, wkv_p[None], shift_prompt, wkv_s[None], shift_sample)
```

```python
import functools
import math

import jax
import jax.numpy as jnp
from jax import lax
from jax.experimental import pallas as pl
from jax.experimental.pallas import tpu as pltpu

F32 = jnp.float32
BF16 = jnp.bfloat16
HIGHEST = lax.Precision.HIGHEST

N_META = 16
HEADS = 8
HEAD_DIM = 64
WIDTH = HEADS * HEAD_DIM
DECAY_LORA = 64
AAA_LORA = 64
GATE_LORA = 160
ATTN_SCALE = 1.0 / math.sqrt(HEAD_DIM)
LOG2E = math.log2(math.e)
NEG_INF = -1e30
GN_EPS = 64e-5
NORM_EPS = 1e-5
TOP_K = 4
SWIGLU_ALPHA = 1.702
SWIGLU_LIMIT = 7.0

LANES = 128
QKV_COLS = 3 * WIDTH
F_OFF = QKV_COLS
F_COLS = LANES
R_OFF = F_OFF + F_COLS
ZW_OFF = 3 * WIDTH
ZA_OFF = ZW_OFF + LANES
ZG_OFF = ZA_OFF + LANES
ZG_COLS = 2 * LANES
R_COLS = ZG_OFF + ZG_COLS
G_OFF = R_OFF + R_COLS
VMEM_LIMIT = 56 * 1024 * 1024


def _cparams(sem, flags=None):
    return pltpu.CompilerParams(dimension_semantics=sem, vmem_limit_bytes=VMEM_LIMIT, flags=flags)


def _full(shape):
    n = len(shape)
    return pl.BlockSpec(shape, lambda *_: (0,) * n)


def _log_sigmoid(x):
    return jnp.minimum(x, 0.0) - jnp.log1p(jnp.exp(-jnp.abs(x)))


def _sigmoid(x):
    return 1.0 / (1.0 + jnp.exp(-x))


def _rms(x, g):
    return x * lax.rsqrt(jnp.mean(x * x, axis=-1, keepdims=True) + NORM_EPS) * g


def _dot(a, b, prec=None):
    return jnp.dot(a, b, preferred_element_type=F32, precision=prec)


def _dot_nt(a, b, prec=None):
    return lax.dot_general(a, b, (((1,), (1,)), ((), ())), preferred_element_type=F32, precision=prec)


def _dot_tn(a, b):
    return _dot(a.T.astype(BF16), b)


def _split_dot(m, x, terms, left=True):
    acc = None
    for _ in range(terms):
        hi = x.astype(BF16)
        part = _dot(m, hi) if left else _dot(hi, m)
        acc = part if acc is None else acc + part
        x = x - hi.astype(F32)
    return acc


def _inproj_kernel(x_ref, g_ref, w_ref, bf_ref, qb_ref, kb_ref, vb_ref, k_ref, v_ref, lf_ref, zr_ref, sg_ref):
    h = _rms(x_ref[...], g_ref[...]).astype(BF16)

    def mm(lo, n):
        return _dot(h, w_ref[:, lo:lo + n])

    qb_ref[...] = (mm(0, WIDTH) * (ATTN_SCALE * LOG2E)).astype(BF16)
    k = mm(WIDTH, WIDTH)
    k_ref[...] = k
    kb_ref[...] = k.astype(BF16)
    v = mm(2 * WIDTH, WIDTH)
    v_ref[...] = v
    vb_ref[...] = v.astype(BF16)
    lf_ref[...] = _log_sigmoid(mm(F_OFF, F_COLS) + bf_ref[...])
    for c in range(R_COLS // WIDTH):
        zr_ref[:, c * WIDTH:(c + 1) * WIDTH] = mm(R_OFF + c * WIDTH, WIDTH)
    for c in range(2 * 1024 // WIDTH):
        sg_ref[:, c * WIDTH:(c + 1) * WIDTH] = _sigmoid(mm(G_OFF + c * WIDTH, WIDTH)).astype(BF16)


def _inproj(x, g, w_packed, bf_pad, tm):
    m, d = x.shape
    ncols = w_packed.shape[1]
    row = lambda n: pl.BlockSpec((tm, n), lambda i: (i, 0))
    outs = [
        jax.ShapeDtypeStruct((m, WIDTH), BF16), jax.ShapeDtypeStruct((m, WIDTH), BF16),
        jax.ShapeDtypeStruct((m, WIDTH), BF16), jax.ShapeDtypeStruct((m, WIDTH), F32),
        jax.ShapeDtypeStruct((m, WIDTH), F32), jax.ShapeDtypeStruct((m, F_COLS), F32),
        jax.ShapeDtypeStruct((m, R_COLS), F32), jax.ShapeDtypeStruct((m, 2 * d), BF16),
    ]
    return pl.pallas_call(
        _inproj_kernel,
        out_shape=outs,
        grid=(pl.cdiv(m, tm),),
        in_specs=[row(d), _full((1, d)), _full((d, ncols)), _full((1, F_COLS))],
        out_specs=[row(WIDTH), row(WIDTH), row(WIDTH), row(WIDTH), row(WIDTH), row(F_COLS), row(R_COLS),
                   row(2 * d)],
        compiler_params=_cparams(("parallel",)),
        name="inproj",
    )(x, g, w_packed, bf_pad)


CUM_BLOCK = 256


def _cumsum_kernel(lf_ref, c_ref):
    length = lf_ref.shape[1]
    r = lax.broadcasted_iota(jnp.int32, (CUM_BLOCK, CUM_BLOCK), 0)
    c = lax.broadcasted_iota(jnp.int32, (CUM_BLOCK, CUM_BLOCK), 1)
    tri = (c <= r).astype(F32)
    carry = jnp.zeros((1, lf_ref.shape[2]), F32)
    for lo in range(0, length, CUM_BLOCK):
        n = min(CUM_BLOCK, length - lo)
        blk = _dot(tri[:n, :n], lf_ref[0, lo:lo + n, :], HIGHEST) + carry
        c_ref[0, lo:lo + n, :] = blk * LOG2E
        carry = blk[n - 1:n, :]


def _cumsum_logf(lf):
    b, length, n = lf.shape
    spec = pl.BlockSpec((1, length, n), lambda i: (i, 0, 0))
    return pl.pallas_call(
        _cumsum_kernel,
        out_shape=jax.ShapeDtypeStruct(lf.shape, F32),
        grid=(b,),
        in_specs=[spec],
        out_specs=spec,
        compiler_params=_cparams(("parallel",)),
        name="cumsum_logf",
    )(lf)


QB = 256
PAIR = 2
ATTN_GROUPS = 2


def _attn_kernel(q_ref, k_ref, v_ref, ct_ref, o_ref, *, n_full, tail):
    step_id = pl.program_id(1)
    lane_head = lax.broadcasted_iota(jnp.int32, (1, LANES), 1) // HEAD_DIM
    one = jnp.ones((), BF16)
    groups = range(ATTN_GROUPS)
    gl = lambda gi: slice(gi * LANES, (gi + 1) * LANES)

    def attend(q_rows, rows, n_off, diag_rows, diag_lo):
        qs = []
        for gi in groups:
            qt = q_ref[0, rows, gl(gi)]
            qs.append(jnp.concatenate([jnp.where(lane_head == hh, qt, jnp.zeros_like(qt)) for hh in range(PAIR)],
                                      axis=0))

        def step(kv, c0, carry, masked):
            s_all = [_dot_nt(qs[gi], kv[gi][0]) for gi in groups]
            soft = []
            for gi in groups:
                for hh in range(PAIR):
                    m, _ = carry[gi * PAIR + hh]
                    head = (step_id * ATTN_GROUPS + gi) * PAIR + hh
                    s = s_all[gi][hh * q_rows:(hh + 1) * q_rows] - ct_ref[0, head, :, pl.ds(c0, QB)]
                    if masked:
                        ri = lax.broadcasted_iota(jnp.int32, (q_rows, QB), 0)
                        ci = lax.broadcasted_iota(jnp.int32, (q_rows, QB), 1)
                        s = jnp.where(ci <= ri, s, NEG_INF)
                    m_new = jnp.maximum(m, jnp.max(s, axis=-1, keepdims=True))
                    soft.append((m_new, jnp.exp2(m - m_new), jnp.exp2(s - m_new).astype(BF16)))
            new = []
            for gi in groups:
                for hh in range(PAIR):
                    m_new, alpha, p = soft[gi * PAIR + hh]
                    pv = _dot(p, jnp.where(lane_head == hh, kv[gi][1], one))
                    new.append((m_new, alpha * carry[gi * PAIR + hh][1] + pv))
            return tuple(new)

        def body(j, carry):
            c0 = pl.multiple_of(j * QB, QB)
            kv = [(k_ref[0, pl.ds(c0, QB), gl(gi)], v_ref[0, pl.ds(c0, QB), gl(gi)]) for gi in groups]
            return step(kv, c0, carry, False)

        init = tuple((jnp.full((q_rows, 1), NEG_INF, F32), jnp.zeros((q_rows, LANES), F32))
                     for _ in range(ATTN_GROUPS * PAIR))
        carry = step(diag_rows, diag_lo, lax.fori_loop(0, n_off, body, init), True)
        for gi in groups:
            outs = [acc / pltpu.roll(acc, HEAD_DIM, axis=1) for _, acc in carry[gi * PAIR:(gi + 1) * PAIR]]
            o_ref[0, rows, gl(gi)] = jnp.where(lane_head == 0, outs[0], outs[1]).astype(o_ref.dtype)

    def qblock(i, _):
        r0 = pl.multiple_of(i * QB, QB)
        rows = pl.ds(r0, QB)
        attend(QB, rows, i, [(k_ref[0, rows, gl(gi)], v_ref[0, rows, gl(gi)]) for gi in groups], r0)
        return 0

    lax.fori_loop(0, n_full, qblock, 0)
    if tail:
        lo = n_full * QB
        pad = jnp.zeros((QB - tail, LANES), k_ref.dtype)
        padded = lambda ref, gi: jnp.concatenate([ref[0, lo:lo + tail, gl(gi)], pad], axis=0)
        attend(tail, slice(lo, lo + tail), n_full, [(padded(k_ref, gi), padded(v_ref, gi)) for gi in groups], lo)


def _fox_prompt(qb, kb, vb, c_t):
    b, length, _ = qb.shape
    n_full, tail = divmod(length, QB)
    assert tail % 16 == 0 and c_t.shape[3] == (n_full + (1 if tail else 0)) * QB
    qkv = pl.BlockSpec((1, length, LANES * ATTN_GROUPS), lambda i, g: (i, 0, g))
    return pl.pallas_call(
        functools.partial(_attn_kernel, n_full=n_full, tail=tail),
        out_shape=jax.ShapeDtypeStruct(qb.shape, BF16),
        grid=(b, HEADS // (PAIR * ATTN_GROUPS)),
        in_specs=[qkv, qkv, qkv, pl.BlockSpec((1, HEADS, 1, c_t.shape[3]), lambda i, g: (i, 0, 0, 0))],
        out_specs=qkv,
        compiler_params=_cparams(("parallel", "arbitrary")),
        name="fox_prompt",
    )(qb, kb, vb, c_t)


def _head_sum_matrix():
    r = lax.broadcasted_iota(jnp.int32, (WIDTH, WIDTH), 0) // HEAD_DIM
    c = lax.broadcasted_iota(jnp.int32, (WIDTH, WIDTH), 1) // HEAD_DIM
    return (r == c).astype(BF16)


def _rwkv_pointwise(zs, p):
    (w0, w2, a0, a2, g2, k_k, k_a, r_k) = p
    r = zs[:, 0:WIDTH]
    k = zs[:, WIDTH:2 * WIDTH]
    v = zs[:, 2 * WIDTH:3 * WIDTH]
    zw = zs[:, ZW_OFF:ZW_OFF + LANES]
    za = zs[:, ZA_OFF:ZA_OFF + LANES]
    zg = zs[:, ZG_OFF:ZG_OFF + ZG_COLS]
    w_log = _log_sigmoid(w0 + _dot(jnp.tanh(zw).astype(BF16), w2)) - 0.5
    logdec = -jnp.exp(w_log)
    a = _sigmoid(a0 + _dot(za.astype(BF16), a2))
    g = _dot(_sigmoid(zg).astype(BF16), g2)
    hs = _head_sum_matrix()
    kk = k * k_k
    norm = jnp.sqrt(_split_dot(hs, kk * kk, 2, left=False))
    kk = kk / jnp.maximum(norm, 1e-12)
    km = k * (1.0 + (a - 1.0) * k_a)
    bonus = _split_dot(hs, r * km * r_k, 2, left=False) * v
    return r, km, v, kk, a, logdec, g, bonus


def _group_norm_out(y, ln_w, ln_b, bonus, g):
    mean = jnp.mean(y, axis=-1, keepdims=True)
    var = jnp.mean(jnp.square(y - mean), axis=-1, keepdims=True)
    yn = (y - mean) * lax.rsqrt(var + GN_EPS) * ln_w + ln_b
    return (yn + bonus) * g


RCHUNK = 128


def _rwkv_chunk_kernel(zr_ref, zp_ref, sh_ref, mu_ref, w0_ref, w2_ref, a0_ref, a2_ref, g2_ref, kk_ref, ka_ref,
                       rk_ref, lnw_ref, lnb_ref, out_ref, wkv_ref, s_ref, *, length):
    ci = pl.program_id(1)
    n = RCHUNK

    @pl.when(ci == 0)
    def _():
        s_ref[...] = jnp.zeros_like(s_ref)

    zr = zr_ref[0]
    row = lax.broadcasted_iota(jnp.int32, (n, 1), 0)
    prev = jnp.where(ci == 0, sh_ref[0], zp_ref[0, 7:8, :])
    zprev = jnp.where(row == 0, prev, pltpu.roll(zr, 1, axis=0))
    valid = (ci * n + row) < length
    zs = jnp.where(valid, zr + (zprev - zr) * mu_ref[...], 0.0)
    params = (w0_ref[...], w2_ref[...], a0_ref[...], a2_ref[...], g2_ref[...], kk_ref[...], ka_ref[...],
              rk_ref[...])
    r, km, v, kk, a, logdec, g, bonus = _rwkv_pointwise(zs, params)
    logdec = jnp.where(valid, logdec, 0.0)

    ri = lax.broadcasted_iota(jnp.int32, (n, n), 0)
    cj = lax.broadcasted_iota(jnp.int32, (n, n), 1)
    incl = cj <= ri
    strict = cj < ri
    eye = (ri == cj).astype(F32)
    eye_h = eye[:HEAD_DIM, :HEAD_DIM]
    logp = _split_dot(incl.astype(BF16), logdec, 3)
    logp_c = logp[n - 1:n, :]
    e_pos = jnp.exp(logp)
    e_neg = jnp.exp(-logp)
    e_end = jnp.exp(logp_c - logp)
    rt_all = r * e_pos
    pc_all = jnp.exp(logp_c)
    heads = range(HEADS)
    sls = [slice(h * HEAD_DIM, (h + 1) * HEAD_DIM) for h in heads]
    cut = lambda x: [x[:, sl].astype(BF16) for sl in sls]
    rt, kt, at, bt = cut(rt_all), cut(km * e_neg), cut(-kk * jnp.exp(logp - logdec)), cut(kk * a * e_neg)
    bh, kh, vh = cut(kk * a * e_end), cut(km * e_end), cut(v)

    m_all = [_dot_nt(jnp.concatenate([at[h], rt[h]], axis=0), jnp.concatenate([bt[h], kt[h]], axis=0)) for h in heads]
    a_ab = [jnp.where(strict, m[:n, :n], 0.0) for m in m_all]
    a_ak = [jnp.where(strict, m[:n, n:], 0.0).astype(BF16) for m in m_all]
    m_rb = [jnp.where(incl, m[n:, :n], 0.0).astype(BF16) for m in m_all]
    m_rk = [jnp.where(incl, m[n:, n:], 0.0).astype(BF16) for m in m_all]
    t_inv = [eye for _ in heads]
    pw = a_ab
    span = 1
    while span < n:
        pt = [_dot(pw[h].astype(BF16), jnp.concatenate([pw[h], t_inv[h]], axis=1).astype(BF16)) for h in heads]
        pw = [x[:, :n] for x in pt]
        t_inv = [t_inv[h] + pt[h][:, n:] for h in heads]
        span *= 2
    t_bf = [x.astype(BF16) for x in t_inv]
    w_m = [_dot(t_bf[h], at[h]) for h in heads]
    akv = [_dot(a_ak[h], vh[h]).astype(BF16) for h in heads]
    u0 = [_dot(t_bf[h], akv[h]) for h in heads]
    w_bf = [x.astype(BF16) for x in w_m]
    u_bf = [x.astype(BF16) for x in u0]
    qp = [rt_all[:, sls[h]] + _dot(m_rb[h], w_bf[h]) for h in heads]
    y0 = [_dot(m_rb[h], u_bf[h]) + _dot(m_rk[h], vh[h]) for h in heads]
    gt = [eye_h * pc_all[:, sls[h]] + _dot_tn(w_m[h], bh[h]) for h in heads]
    et = [_dot_tn(u0[h], bh[h]) + _dot_tn(v[:, sls[h]], kh[h]) for h in heads]
    for h in heads:
        sl = sls[h]
        s_h = s_ref[h].astype(BF16)
        y = _dot_nt(qp[h].astype(BF16), s_h) + y0[h]
        s_ref[h] = _dot(s_h, gt[h].astype(BF16)) + et[h]
        out_ref[0, :, sl] = _group_norm_out(y, lnw_ref[:, sl], lnb_ref[:, sl], bonus[:, sl],
                                            g[:, sl]).astype(out_ref.dtype)

    @pl.when(ci == pl.num_programs(1) - 1)
    def _():
        wkv_ref[0] = s_ref[...]


def _rwkv_prompt(zr, shift0, rp):
    b, length, _ = zr.shape
    n = RCHUNK
    nchunk = pl.cdiv(length, n)
    vec = _full((1, WIDTH))
    return pl.pallas_call(
        functools.partial(_rwkv_chunk_kernel, length=length),
        out_shape=[jax.ShapeDtypeStruct((b, length, WIDTH), BF16),
                   jax.ShapeDtypeStruct((b, HEADS, HEAD_DIM, HEAD_DIM), F32)],
        grid=(b, nchunk),
        in_specs=[pl.BlockSpec((1, n, R_COLS), lambda i, c: (i, c, 0)),
                  pl.BlockSpec((1, 8, R_COLS), lambda i, c: (i, jnp.maximum(c * (n // 8) - 1, 0), 0)),
                  pl.BlockSpec((1, 1, R_COLS), lambda i, c: (i, 0, 0)),
                  _full((1, R_COLS)), vec, _full((LANES, WIDTH)), vec, _full((LANES, WIDTH)),
                  _full((ZG_COLS, WIDTH)), vec, vec, vec, vec, vec],
        out_specs=[pl.BlockSpec((1, n, WIDTH), lambda i, c: (i, c, 0)),
                   pl.BlockSpec((1, HEADS, HEAD_DIM, HEAD_DIM), lambda i, c: (i, 0, 0, 0))],
        scratch_shapes=[pltpu.VMEM((HEADS, HEAD_DIM, HEAD_DIM), F32)],
        compiler_params=_cparams(("parallel", "arbitrary")),
        name="rwkv_prompt",
    )(zr, zr, shift0, rp["mu"], rp["w0"], rp["w2"], rp["a0"], rp["a2"], rp["g2"], rp["k_k"], rp["k_a"],
      rp["r_k"], rp["ln_w"], rp["ln_b"])


DEC_ROWS = 8


def _rwkv_decode_kernel(zr_ref, sh_ref, st_ref, mu_ref, w0_ref, w2_ref, a0_ref, a2_ref, g2_ref, kk_ref, ka_ref,
                        rk_ref, lnw_ref, lnb_ref, out_ref, ns_ref, vec_ref):
    zr = zr_ref[...]
    zs = zr + (sh_ref[...] - zr) * mu_ref[...]
    params = (w0_ref[...], w2_ref[...], a0_ref[...], a2_ref[...], g2_ref[...], kk_ref[...], ka_ref[...],
              rk_ref[...])
    r, km, v, kk, a, logdec, g, bonus = _rwkv_pointwise(zs, params)
    for idx, x in enumerate((r, km, v, kk, kk * a, jnp.exp(logdec), g, bonus)):
        for s in range(DEC_ROWS):
            vec_ref[idx, s] = x[s:s + 1, :]
    ri = lax.broadcasted_iota(jnp.int32, (HEAD_DIM, HEAD_DIM), 0)
    cj = lax.broadcasted_iota(jnp.int32, (HEAD_DIM, HEAD_DIM), 1)
    eye = ri == cj

    def per_row(s, _):
        for h in range(HEADS):
            sl = slice(h * HEAD_DIM, (h + 1) * HEAD_DIM)
            r_h, k_h, v_h, kk_h, ka_h, w_h, g_h, bo_h = (vec_ref[i, s, :, sl] for i in range(8))
            st = st_ref[s, h]
            sa = jnp.sum(st * (-kk_h), axis=-1, keepdims=True)
            v_col = jnp.sum(jnp.where(eye, v_h, 0.0), axis=-1, keepdims=True)
            st = st * w_h + sa * ka_h + v_col * k_h
            ns_ref[s, h] = st
            y_col = jnp.sum(st * r_h, axis=-1, keepdims=True)
            y = jnp.sum(jnp.where(eye, y_col, 0.0), axis=0, keepdims=True)
            out_ref[s, :, sl] = _group_norm_out(y, lnw_ref[:, sl], lnb_ref[:, sl], bo_h, g_h)
        return 0

    lax.fori_loop(0, DEC_ROWS, per_row, 0)


def _rwkv_decode(zr, shift, state, rp):
    n = zr.shape[0]
    vec = _full((1, WIDTH))
    rows = lambda c: pl.BlockSpec((DEC_ROWS, c), lambda i: (i, 0))
    st = pl.BlockSpec((DEC_ROWS, HEADS, HEAD_DIM, HEAD_DIM), lambda i: (i, 0, 0, 0))
    return pl.pallas_call(
        _rwkv_decode_kernel,
        out_shape=[jax.ShapeDtypeStruct((n, 1, WIDTH), F32), jax.ShapeDtypeStruct(state.shape, F32)],
        grid=(n // DEC_ROWS,),
        in_specs=[rows(R_COLS), rows(R_COLS), st, _full((1, R_COLS)), vec, _full((LANES, WIDTH)), vec,
                  _full((LANES, WIDTH)), _full((ZG_COLS, WIDTH)), vec, vec, vec, vec, vec],
        out_specs=[pl.BlockSpec((DEC_ROWS, 1, WIDTH), lambda i: (i, 0, 0)), st],
        scratch_shapes=[pltpu.VMEM((8, DEC_ROWS, 1, WIDTH), F32)],
        compiler_params=_cparams(("parallel",)),
        name="rwkv_decode",
    )(zr, shift, state, rp["mu"], rp["w0"], rp["w2"], rp["a0"], rp["a2"], rp["g2"], rp["k_k"], rp["k_a"],
      rp["r_k"], rp["ln_w"], rp["ln_b"])


SUFFIX_PAGES = 512
DEC_PAGES = 16


def _page_suffix_kernel(lf_ref, et_ref):
    pages, heads, plen = lf_ref.shape
    x = lf_ref[...].reshape(pages * heads, plen)
    u = lax.broadcasted_iota(jnp.int32, (plen, plen), 0)
    t = lax.broadcasted_iota(jnp.int32, (plen, plen), 1)
    after = _split_dot((u > t).astype(BF16), x, 3, left=False)
    total = _split_dot(jnp.ones((plen, plen), BF16), x, 3, left=False)
    et_ref[:, 0] = (after * LOG2E).reshape(pages, heads, plen)
    et_ref[:, 1] = (total * LOG2E).reshape(pages, heads, plen)


def _page_suffix(lft):
    n_pool, heads, plen = lft.shape
    return pl.pallas_call(
        _page_suffix_kernel,
        out_shape=jax.ShapeDtypeStruct((n_pool, 2, heads, plen), F32),
        grid=(pl.cdiv(n_pool, SUFFIX_PAGES),),
        in_specs=[pl.BlockSpec((SUFFIX_PAGES, heads, plen), lambda i: (i, 0, 0))],
        out_specs=pl.BlockSpec((SUFFIX_PAGES, 2, heads, plen), lambda i: (i, 0, 0, 0)),
        compiler_params=_cparams(("parallel",)),
        name="page_suffix",
    )(lft)


def _fox_decode_kernel(pt_ref, q_ref, kn_ref, vn_ref, lfn_ref, *refs):
    del pt_ref
    k_refs, v_refs, et_refs = (refs[i * DEC_PAGES:(i + 1) * DEC_PAGES] for i in range(3))
    o_ref, m_ref, l_ref, acc_ref, carry_ref = refs[3 * DEC_PAGES:]
    g = pl.program_id(1)
    plen = k_refs[0].shape[4]

    @pl.when(g == 0)
    def _():
        m_ref[...] = jnp.full_like(m_ref, NEG_INF)
        l_ref[...] = jnp.zeros_like(l_ref)
        acc_ref[...] = jnp.zeros_like(acc_ref)
        carry_ref[...] = lfn_ref[0] * LOG2E

    for h in range(HEADS):
        row = slice(h, h + 1)
        q_b = jnp.broadcast_to(q_ref[0, h], (HEAD_DIM, plen))
        carry = carry_ref[row, :]
        scores = []
        for i in range(DEC_PAGES):
            s = jnp.sum(k_refs[i][0, 0, h] * q_b, axis=0, keepdims=True)
            scores.append(s + et_refs[i][0, 0, row, :] + carry)
            carry = carry + et_refs[i][0, 1, row, 0:1]
        carry_ref[row, :] = carry
        top = scores[0]
        for s in scores[1:]:
            top = jnp.maximum(top, s)
        m_old = m_ref[row, :]
        m_new = jnp.maximum(m_old, jnp.max(top, axis=-1, keepdims=True))
        alpha = jnp.exp2(m_old - m_new)
        m_ref[row, :] = m_new
        psum = None
        pv = None
        for i in range(DEC_PAGES):
            pr = jnp.exp2(scores[i] - m_new)
            part = v_refs[i][0, 0, h] * pr
            psum = pr if psum is None else psum + pr
            pv = part if pv is None else pv + part
        l_ref[row, :] = alpha * l_ref[row, :] + jnp.sum(psum, axis=-1, keepdims=True)
        acc_ref[h] = alpha * acc_ref[h] + pv

    @pl.when(g == pl.num_programs(1) - 1)
    def _():
        for h in range(HEADS):
            row = slice(h, h + 1)
            s_new = jnp.sum(q_ref[0, h] * kn_ref[0, h], axis=0, keepdims=True)
            m_fin = jnp.maximum(m_ref[row, :], s_new)
            a_fin = jnp.exp2(m_ref[row, :] - m_fin)
            p_new = jnp.exp2(s_new - m_fin)
            num = a_fin * jnp.sum(acc_ref[h], axis=-1, keepdims=True) + p_new * vn_ref[0, h]
            o_ref[0, h] = num / (a_fin * l_ref[row, :] + p_new)


def _fox_decode(page_table, q, k_new, v_new, lf_new, cache_kt, cache_vt, et):
    n, n_pages = page_table.shape
    page = cache_kt.shape[4]
    assert n_pages % DEC_PAGES == 0
    tok = pl.BlockSpec((1, HEADS, HEAD_DIM, 1), lambda s, g, pt: (s, 0, 0, 0))

    def paged(shape, i, lead):
        idx = lambda s, g, pt: lead + (pt[s, n_pages - 1 - (g * DEC_PAGES + i)],) + (0,) * (len(shape) - len(lead) - 1)
        return pl.BlockSpec(shape, idx)

    slots = range(DEC_PAGES)
    kv_shape = (1, 1, HEADS, HEAD_DIM, page)
    grid_spec = pltpu.PrefetchScalarGridSpec(
        num_scalar_prefetch=1,
        grid=(n, n_pages // DEC_PAGES),
        in_specs=[tok, tok, tok, pl.BlockSpec((1, HEADS, 1), lambda s, g, pt: (s, 0, 0))]
        + [paged(kv_shape, i, (0,)) for i in slots] + [paged(kv_shape, i, (0,)) for i in slots]
        + [paged((1, 2, HEADS, page), i, ()) for i in slots],
        out_specs=tok,
        scratch_shapes=[pltpu.VMEM((HEADS, 1), F32), pltpu.VMEM((HEADS, 1), F32),
                        pltpu.VMEM((HEADS, HEAD_DIM, page), F32), pltpu.VMEM((HEADS, 1), F32)],
    )
    return pl.pallas_call(
        _fox_decode_kernel,
        out_shape=jax.ShapeDtypeStruct((n, HEADS, HEAD_DIM, 1), F32),
        grid_spec=grid_spec,
        compiler_params=_cparams(("parallel", "arbitrary")),
        name="fox_decode",
    )(page_table, q, k_new, v_new, lf_new, *([cache_kt] * DEC_PAGES), *([cache_vt] * DEC_PAGES), *([et] * DEC_PAGES))


def _outproj_kernel(oa_ref, orw_ref, sg_ref, x_ref, woa_ref, wor_ref, wout_ref, nf_ref, wrt_ref, brc_ref, x1_ref,
                    h2t_ref, gate_ref, gate3_ref, *, n_valid):
    i = pl.program_id(0)
    d = x_ref.shape[1]
    n_exp = wrt_ref.shape[0]

    @pl.when(i < n_valid)
    def _():
        a = _dot(oa_ref[...], woa_ref[...])
        b = _dot(orw_ref[...], wor_ref[...])
        merged = sg_ref[:, :d].astype(F32) * a + sg_ref[:, d:].astype(F32) * b
        x1 = x_ref[...] + _dot(merged.astype(BF16), wout_ref[...])
        x1_ref[...] = x1
        h2 = _rms(x1, nf_ref[...])
        h2t_ref[...] = h2.T.astype(BF16)
        work = _dot_nt(wrt_ref[...], h2, HIGHEST) + brc_ref[...]
        row = lax.broadcasted_iota(jnp.int32, work.shape, 0)
        top = None
        denom = jnp.zeros((1, work.shape[1]), F32)
        picks = []
        for _ in range(TOP_K):
            mx = jnp.max(work, axis=0, keepdims=True)
            idx = jnp.min(jnp.where(work == mx, row, n_exp), axis=0, keepdims=True)
            hit = row == idx
            top = mx if top is None else top
            e = jnp.exp(mx - top)
            denom = denom + e
            picks.append((hit, e))
            work = jnp.where(hit, -jnp.inf, work)
        gate = jnp.zeros(work.shape, F32)
        for hit, e in picks:
            gate = jnp.where(hit, e / denom, gate)
        gate_ref[...] = gate
        for ex in range(n_exp):
            gate3_ref[ex] = gate[ex:ex + 1, :]

    @pl.when(i >= n_valid)
    def _():
        x1_ref[...] = jnp.zeros_like(x1_ref)
        h2t_ref[...] = jnp.zeros_like(h2t_ref)
        gate_ref[...] = jnp.zeros_like(gate_ref)
        gate3_ref[...] = jnp.zeros_like(gate3_ref)


def _outproj(o_attn, o_rwkv, sg, x, w_oa, w_or, w_out, norm_ffn, w_router_t, b_router_col, tm, m_pad):
    m, d = x.shape
    n_exp = w_router_t.shape[0]
    n_valid = m // tm
    assert m % tm == 0 and m_pad % tm == 0
    last = n_valid - 1
    row = lambda n: pl.BlockSpec((tm, n), lambda i: (jnp.minimum(i, last), 0))
    return pl.pallas_call(
        functools.partial(_outproj_kernel, n_valid=n_valid),
        out_shape=[jax.ShapeDtypeStruct((m_pad, d), F32), jax.ShapeDtypeStruct((d, m_pad), BF16),
                   jax.ShapeDtypeStruct((n_exp, m_pad), F32), jax.ShapeDtypeStruct((n_exp, 1, m_pad), F32)],
        grid=(m_pad // tm,),
        in_specs=[row(WIDTH), row(WIDTH), row(2 * d), row(d), _full((WIDTH, d)), _full((WIDTH, d)), _full((d, d)),
                  _full((1, d)), _full((n_exp, d)), _full((n_exp, 1))],
        out_specs=[pl.BlockSpec((tm, d), lambda i: (i, 0)), pl.BlockSpec((d, tm), lambda i: (0, i)),
                   pl.BlockSpec((n_exp, tm), lambda i: (0, i)), pl.BlockSpec((n_exp, 1, tm), lambda i: (0, 0, i))],
        compiler_params=_cparams(("parallel",)),
        name="outproj_router",
    )(o_attn, o_rwkv, sg, x, w_oa, w_or, w_out, norm_ffn, w_router_t, b_router_col)


MOE_TOKENS = 1792
MOE_SLOTS = 256
MOE_FF_CHUNK = 256


def _moe_kernel(h2t_ref, gate_ref, gate3_ref, wu_ref, bu_ref, wd_ref, bd_ref, o_ref, acc_ref, rank_ref, xg_ref,
                sg_ref, dn_ref):
    e = pl.program_id(1)
    last = pl.num_programs(1) - 1
    d, tt = h2t_ref.shape
    n_exp = gate_ref.shape[0]
    d_ff = wd_ref.shape[2]
    reps = MOE_SLOTS // LANES
    cols = lambda ref, lo, n: jnp.concatenate([ref[0, lo:lo + n, :]] * reps, axis=1)

    def gather(ex, blk):
        g_row = gate3_ref[ex]
        slot_of = jnp.where(g_row > 0.0, rank_ref[ex], -1.0)
        slot = (blk * MOE_SLOTS + lax.broadcasted_iota(jnp.int32, (MOE_SLOTS, 1), 0)).astype(F32)
        sel = slot_of == slot
        sel_one = jnp.where(sel, 1.0, 0.0).astype(BF16)
        return _dot_nt(h2t_ref[...], sel_one).astype(BF16), jnp.where(sel, g_row, 0.0).astype(BF16)

    def ffn(xg):
        def up(lo):
            return (_dot(wu_ref[0, lo:lo + MOE_FF_CHUNK, :], xg),
                    _dot(wu_ref[0, d_ff + lo:d_ff + lo + MOE_FF_CHUNK, :], xg))

        dn = None
        starts = list(range(0, d_ff, MOE_FF_CHUNK))
        ahead = up(starts[0])
        for ci, lo in enumerate(starts):
            x_glu, x_lin = ahead
            if ci + 1 < len(starts):
                ahead = up(starts[ci + 1])
            x_glu = jnp.minimum(x_glu + cols(bu_ref, lo, MOE_FF_CHUNK), SWIGLU_LIMIT)
            x_lin = jnp.clip(x_lin + cols(bu_ref, d_ff + lo, MOE_FF_CHUNK), -SWIGLU_LIMIT, SWIGLU_LIMIT)
            act = (x_glu * _sigmoid(SWIGLU_ALPHA * x_glu) * (x_lin + 1.0)).astype(BF16)
            part = _dot(wd_ref[0, :, lo:lo + MOE_FF_CHUNK], act)
            dn = part if dn is None else dn + part
        return jnp.concatenate([(dn[lo:lo + MOE_FF_CHUNK] + cols(bd_ref, lo, MOE_FF_CHUNK)).astype(BF16)
                                for lo in range(0, d, MOE_FF_CHUNK)], axis=0)

    def scatter(dn, sel_gate):
        for lo in range(0, d, MOE_FF_CHUNK):
            rows = slice(lo, lo + MOE_FF_CHUNK)
            acc_ref[rows, :] += _dot(dn[rows], sel_gate)

    @pl.when(e == 0)
    def _():
        acc_ref[...] = jnp.zeros_like(acc_ref)
        tr = lax.broadcasted_iota(jnp.int32, (tt, tt), 0)
        tc = lax.broadcasted_iota(jnp.int32, (tt, tt), 1)
        hit_all = jnp.where(gate_ref[...] > 0.0, 1.0, 0.0).astype(BF16)
        rank = _dot(hit_all, (tr < tc).astype(BF16))
        for ex in range(n_exp):
            rank_ref[ex] = rank[ex:ex + 1, :]
        xg0, sg0 = gather(0, 0)
        xg_ref[0] = xg0
        sg_ref[0] = sg0
        dn_ref[...] = jnp.zeros_like(dn_ref)
        sg_ref[2] = jnp.zeros_like(sg0)

    scatter(dn_ref[...], sg_ref[(e + 2) % 3])
    dn_new = ffn(xg_ref[e % 2])
    xg_next, sg_next = gather(jnp.minimum(e + 1, last), 0)
    xg_ref[(e + 1) % 2] = xg_next
    sg_ref[(e + 1) % 3] = sg_next
    dn_ref[...] = dn_new

    def extra_pass(blk, _):
        xg, sel_gate = gather(e, blk)
        scatter(ffn(xg), sel_gate)
        return 0

    count = jnp.sum(jnp.where(gate3_ref[e] > 0.0, 1.0, 0.0)).astype(jnp.int32)
    lax.fori_loop(1, (count + MOE_SLOTS - 1) // MOE_SLOTS, extra_pass, 0)

    @pl.when(e == last)
    def _():
        scatter(dn_ref[...], sg_ref[last % 3])
        o_ref[...] = acc_ref[...].T.astype(o_ref.dtype)


def _moe(h2t, gate, gate3, w_up_t, b_up, w_down_t, b_down, tt):
    d, m_pad = h2t.shape
    n_exp, d_up, _ = w_up_t.shape
    assert m_pad % tt == 0
    return pl.pallas_call(
        _moe_kernel,
        out_shape=jax.ShapeDtypeStruct((m_pad, d), BF16),
        grid=(m_pad // tt, n_exp),
        in_specs=[pl.BlockSpec((d, tt), lambda i, e: (0, i)),
                  pl.BlockSpec((n_exp, tt), lambda i, e: (0, i)),
                  pl.BlockSpec((n_exp, 1, tt), lambda i, e: (0, 0, i)),
                  pl.BlockSpec((1, d_up, d), lambda i, e: (e, 0, 0)),
                  pl.BlockSpec((1, d_up, LANES), lambda i, e: (e, 0, 0)),
                  pl.BlockSpec((1, d, w_down_t.shape[2]), lambda i, e: (e, 0, 0)),
                  pl.BlockSpec((1, d, LANES), lambda i, e: (e, 0, 0))],
        out_specs=pl.BlockSpec((tt, d), lambda i, e: (i, 0)),
        scratch_shapes=[pltpu.VMEM((d, tt), F32), pltpu.VMEM((n_exp, 1, tt), F32),
                        pltpu.VMEM((2, d, MOE_SLOTS), BF16), pltpu.VMEM((3, MOE_SLOTS, tt), BF16),
                        pltpu.VMEM((d, MOE_SLOTS), BF16)],
        compiler_params=_cparams(("parallel", "arbitrary")),
        name="moe_routed",
    )(h2t, gate, gate3, w_up_t, b_up, w_down_t, b_down)


def _final_kernel(x1_ref, moe_ref, g_ref, y_ref):
    y_ref[...] = _rms(x1_ref[...] + moe_ref[...].astype(F32), g_ref[...]).reshape(y_ref.shape)


def _final_norm_prompt(x1, moe, g, b, length, skip, tm):
    d = x1.shape[1]
    seq = length - skip
    assert seq % tm == 0 and skip % 8 == 0
    assert length % 8 == 0
    rows = pl.BlockSpec((pl.Element(tm), pl.Element(d)),
                        lambda i, j: (pl.multiple_of(i * length + skip + j * tm, 8), 0))
    return pl.pallas_call(
        _final_kernel,
        out_shape=jax.ShapeDtypeStruct((b, seq, d), F32),
        grid=(b, seq // tm),
        in_specs=[rows, rows, _full((1, d))],
        out_specs=pl.BlockSpec((1, tm, d), lambda i, j: (i, j, 0)),
        compiler_params=_cparams(("parallel", "parallel")),
        name="final_norm_prompt",
    )(x1, moe, g)


def _final_norm(x1, moe, g, m, tm):
    d = x1.shape[1]
    row = pl.BlockSpec((tm, d), lambda i: (i, 0))
    return pl.pallas_call(
        _final_kernel,
        out_shape=jax.ShapeDtypeStruct((m, d), F32),
        grid=(pl.cdiv(m, tm),),
        in_specs=[row, row, _full((1, d))],
        out_specs=row,
        compiler_params=_cparams(("parallel",)),
        name="final_norm",
    )(x1, moe, g)


def _pad_cols(a, n):
    return jnp.pad(a, ((0, 0), (0, n - a.shape[1])))


def _pack_rwkv_cols(a):
    o = 3 * WIDTH
    return jnp.concatenate([
        a[:, :o], _pad_cols(a[:, o:o + DECAY_LORA], LANES),
        _pad_cols(a[:, o + DECAY_LORA:o + DECAY_LORA + AAA_LORA], LANES),
        _pad_cols(a[:, o + DECAY_LORA + AAA_LORA:], ZG_COLS)], axis=1)


def _unpack_rwkv_cols(a):
    return jnp.concatenate([a[..., :ZW_OFF], a[..., ZW_OFF:ZW_OFF + DECAY_LORA], a[..., ZA_OFF:ZA_OFF + AAA_LORA],
                            a[..., ZG_OFF:ZG_OFF + GATE_LORA]], axis=-1)


def _pad_rows(a, n):
    return jnp.pad(a, ((0, n - a.shape[0]), (0, 0)))


def kernel(x_prompt, x_sample, cache_k, cache_v, cache_logf, page_table, state_wkv, state_shift, meta_tokens, norm_mix, w_in, b_f, w_oa, rwkv_mu, rwkv_w0, rwkv_w2, rwkv_a0, rwkv_a2, rwkv_g2, rwkv_k_k, rwkv_k_a, rwkv_r_k, rwkv_ln_w, rwkv_ln_b, w_or, w_out, norm_ffn, w_router, b_router, w_up, b_up, w_down, b_down, norm_final):
    b, seq, d = x_prompt.shape
    db = x_sample.shape[0]
    depth = w_in.shape[0]
    assert depth == 1 and x_sample.shape[1] == 1
    length = N_META + seq
    n_exp = w_router.shape[2]
    a_cols = 3 * WIDTH + HEADS
    r_cols = 3 * WIDTH + DECAY_LORA + AAA_LORA + GATE_LORA

    wi = w_in[0]
    w_packed = jnp.concatenate([
        wi[:, :QKV_COLS], _pad_cols(wi[:, QKV_COLS:a_cols], F_COLS),
        _pack_rwkv_cols(wi[:, a_cols:a_cols + r_cols]), wi[:, a_cols + r_cols:]], axis=1).astype(BF16)
    bf_pad = _pad_cols(b_f[0][None, :], F_COLS)
    rp = {
        "mu": _pack_rwkv_cols(rwkv_mu[0][None, :]),
        "w0": rwkv_w0[0][None, :], "a0": rwkv_a0[0][None, :],
        "w2": _pad_rows(rwkv_w2[0], LANES).astype(BF16), "a2": _pad_rows(rwkv_a2[0], LANES).astype(BF16),
        "g2": _pad_rows(rwkv_g2[0], ZG_COLS).astype(BF16),
        "k_k": rwkv_k_k[0][None, :], "k_a": rwkv_k_a[0][None, :], "r_k": rwkv_r_k[0].reshape(1, WIDTH),
        "ln_w": rwkv_ln_w[0][None, :], "ln_b": rwkv_ln_b[0][None, :],
    }
    g_mix = norm_mix[0][None, :]
    g_ffn = norm_ffn[0][None, :]
    g_fin = norm_final[None, :]
    woa, wor, wout = w_oa[0].astype(BF16), w_or[0].astype(BF16), w_out[0].astype(BF16)
    wr_t = jnp.transpose(w_router[0])
    br_col = b_router[0][:, None]
    wu_t = jnp.transpose(w_up[0], (0, 2, 1)).astype(BF16)
    wd_t = jnp.transpose(w_down[0], (0, 2, 1)).astype(BF16)
    bu = jnp.broadcast_to(b_up[0][:, :, None], b_up.shape[1:] + (LANES,))
    bd = jnp.broadcast_to(b_down[0][:, :, None], b_down.shape[1:] + (LANES,))

    meta = jnp.broadcast_to(meta_tokens.astype(x_prompt.dtype)[None], (b, N_META, d))
    m_p = b * length
    xp = jnp.concatenate([meta, x_prompt], axis=1).reshape(m_p, d)
    qb, kb, vb, k_p, v_p, lf_p, zr_p, sg_p = _inproj(xp, g_mix, w_packed, bf_pad, 512)
    c2 = _cumsum_logf(lf_p.reshape(b, length, F_COLS))
    lpad = pl.cdiv(length, QB) * QB
    c_t = jnp.pad(jnp.transpose(c2[:, :, :HEADS], (0, 2, 1)), ((0, 0), (0, 0), (0, lpad - length)))[:, :, None, :]
    as3 = lambda a: a.reshape(b, length, a.shape[-1])
    o_attn = _fox_prompt(as3(qb), as3(kb), as3(vb), c_t)
    zr3 = as3(zr_p)
    o_rwkv, wkv_p = _rwkv_prompt(zr3, jnp.zeros((b, 1, R_COLS), F32), rp)
    m_pad = pl.cdiv(m_p, MOE_TOKENS) * MOE_TOKENS
    x1_p, h2t_p, gate_p, gate3_p = _outproj(o_attn.reshape(m_p, WIDTH), o_rwkv.reshape(m_p, WIDTH), sg_p, xp, woa,
                                            wor, wout, g_ffn, wr_t, br_col, 256, m_pad)
    moe_p = _moe(h2t_p, gate_p, gate3_p, wu_t, bu, wd_t, bd, MOE_TOKENS)
    y_prompt = _final_norm_prompt(x1_p, moe_p, g_fin, b, length, N_META, 512)

    xs = x_sample.reshape(db, d)
    qs, _, _, k_s, v_s, lf_s, zr_s, sg_s = _inproj(xs, g_mix, w_packed, bf_pad, db)
    col = lambda a: a.reshape(db, HEADS, HEAD_DIM, 1)
    to_lanes = lambda c: jnp.transpose(c, (0, 1, 3, 4, 2))
    et_page = _page_suffix(jnp.transpose(cache_logf[0], (0, 2, 1)))
    o_attn_s = _fox_decode(page_table, col(qs.astype(F32)), col(k_s), col(v_s), lf_s[:, :HEADS, None],
                           to_lanes(cache_k), to_lanes(cache_v), et_page)
    o_rwkv_s, wkv_s = _rwkv_decode(zr_s, _pack_rwkv_cols(state_shift[0, :, 0, :]), state_wkv[0], rp)
    x1_s, h2t_s, gate_s, gate3_s = _outproj(o_attn_s.reshape(db, WIDTH).astype(BF16),
                                            o_rwkv_s.reshape(db, WIDTH).astype(BF16), sg_s, xs, woa, wor, wout, g_ffn,
                                            wr_t, br_col, db, db)
    moe_s = _moe(h2t_s, gate_s, gate3_s, wu_t, bu, wd_t, bd, db)
    y_s = _final_norm(x1_s, moe_s, g_fin, db, db)

    y_sample = y_s.reshape(db, 1, d)
    hd = lambda a, n, t: a.reshape(1, n, t, HEADS, HEAD_DIM)
    logf_prompt = lf_p[:, :HEADS].reshape(1, b, length, HEADS)
    logf_sample = lf_s[:, :HEADS].reshape(1, db, 1, HEADS)
    shift_prompt = _unpack_rwkv_cols(zr3[:, length - 1:length, :])[None]
    shift_sample = _unpack_rwkv_cols(zr_s)[None, :, None, :]
    return (y_prompt, y_sample, hd(k_p, b, length), hd(v_p, b, length), logf_prompt, hd(k_s, db, 1),
            hd(v_s, db, 1), logf_sample, wkv_p[None], shift_prompt, wkv_s[None], shift_sample)
```

```python
import functools
import math

import jax
import jax.numpy as jnp
from jax import lax
from jax.experimental import pallas as pl
from jax.experimental.pallas import tpu as pltpu

F32 = jnp.float32
BF16 = jnp.bfloat16
HIGHEST = lax.Precision.HIGHEST

N_META = 16
HEADS = 8
HEAD_DIM = 64
WIDTH = HEADS * HEAD_DIM
DECAY_LORA = 64
AAA_LORA = 64
GATE_LORA = 160
ATTN_SCALE = 1.0 / math.sqrt(HEAD_DIM)
LOG2E = math.log2(math.e)
NEG_INF = -1e30
GN_EPS = 64e-5
NORM_EPS = 1e-5
TOP_K = 4
SWIGLU_ALPHA = 1.702
SWIGLU_LIMIT = 7.0

LANES = 128
QKV_COLS = 3 * WIDTH
F_OFF = QKV_COLS
F_COLS = LANES
R_OFF = F_OFF + F_COLS
ZW_OFF = 3 * WIDTH
ZA_OFF = ZW_OFF + LANES
ZG_OFF = ZA_OFF + LANES
ZG_COLS = 2 * LANES
R_COLS = ZG_OFF + ZG_COLS
G_OFF = R_OFF + R_COLS
VMEM_LIMIT = 56 * 1024 * 1024


def _cparams(sem, flags=None):
    return pltpu.CompilerParams(dimension_semantics=sem, vmem_limit_bytes=VMEM_LIMIT, flags=flags)


def _full(shape):
    n = len(shape)
    return pl.BlockSpec(shape, lambda *_: (0,) * n)


def _log_sigmoid(x):
    return jnp.minimum(x, 0.0) - jnp.log1p(jnp.exp(-jnp.abs(x)))


def _sigmoid(x):
    return 1.0 / (1.0 + jnp.exp(-x))


def _rms(x, g):
    return x * lax.rsqrt(jnp.mean(x * x, axis=-1, keepdims=True) + NORM_EPS) * g


def _dot(a, b, prec=None):
    return jnp.dot(a, b, preferred_element_type=F32, precision=prec)


def _dot_nt(a, b, prec=None):
    return lax.dot_general(a, b, (((1,), (1,)), ((), ())), preferred_element_type=F32, precision=prec)


def _dot_tn(a, b):
    return _dot(a.T.astype(BF16), b)


def _split_dot(m, x, terms, left=True):
    acc = None
    for _ in range(terms):
        hi = x.astype(BF16)
        part = _dot(m, hi) if left else _dot(hi, m)
        acc = part if acc is None else acc + part
        x = x - hi.astype(F32)
    return acc


def _inproj_kernel(x_ref, g_ref, w_ref, bf_ref, qb_ref, kb_ref, vb_ref, k_ref, v_ref, lf_ref, zr_ref, sg_ref):
    h = _rms(x_ref[...], g_ref[...]).astype(BF16)

    def mm(lo, n):
        return _dot(h, w_ref[:, lo:lo + n])

    qb_ref[...] = (mm(0, WIDTH) * (ATTN_SCALE * LOG2E)).astype(BF16)
    k = mm(WIDTH, WIDTH)
    k_ref[...] = k
    kb_ref[...] = k.astype(BF16)
    v = mm(2 * WIDTH, WIDTH)
    v_ref[...] = v
    vb_ref[...] = v.astype(BF16)
    lf_ref[...] = _log_sigmoid(mm(F_OFF, F_COLS) + bf_ref[...])
    for c in range(R_COLS // WIDTH):
        zr_ref[:, c * WIDTH:(c + 1) * WIDTH] = mm(R_OFF + c * WIDTH, WIDTH)
    for c in range(2 * 1024 // WIDTH):
        sg_ref[:, c * WIDTH:(c + 1) * WIDTH] = _sigmoid(mm(G_OFF + c * WIDTH, WIDTH)).astype(BF16)


def _inproj(x, g, w_packed, bf_pad, tm):
    m, d = x.shape
    ncols = w_packed.shape[1]
    row = lambda n: pl.BlockSpec((tm, n), lambda i: (i, 0))
    outs = [
        jax.ShapeDtypeStruct((m, WIDTH), BF16), jax.ShapeDtypeStruct((m, WIDTH), BF16),
        jax.ShapeDtypeStruct((m, WIDTH), BF16), jax.ShapeDtypeStruct((m, WIDTH), F32),
        jax.ShapeDtypeStruct((m, WIDTH), F32), jax.ShapeDtypeStruct((m, F_COLS), F32),
        jax.ShapeDtypeStruct((m, R_COLS), F32), jax.ShapeDtypeStruct((m, 2 * d), BF16),
    ]
    return pl.pallas_call(
        _inproj_kernel,
        out_shape=outs,
        grid=(pl.cdiv(m, tm),),
        in_specs=[row(d), _full((1, d)), _full((d, ncols)), _full((1, F_COLS))],
        out_specs=[row(WIDTH), row(WIDTH), row(WIDTH), row(WIDTH), row(WIDTH), row(F_COLS), row(R_COLS),
                   row(2 * d)],
        compiler_params=_cparams(("parallel",)),
        name="inproj",
    )(x, g, w_packed, bf_pad)


CUM_BLOCK = 256


def _cumsum_kernel(lf_ref, c_ref):
    length = lf_ref.shape[1]
    r = lax.broadcasted_iota(jnp.int32, (CUM_BLOCK, CUM_BLOCK), 0)
    c = lax.broadcasted_iota(jnp.int32, (CUM_BLOCK, CUM_BLOCK), 1)
    tri = (c <= r).astype(F32)
    carry = jnp.zeros((1, lf_ref.shape[2]), F32)
    for lo in range(0, length, CUM_BLOCK):
        n = min(CUM_BLOCK, length - lo)
        blk = _dot(tri[:n, :n], lf_ref[0, lo:lo + n, :], HIGHEST) + carry
        c_ref[0, lo:lo + n, :] = blk * LOG2E
        carry = blk[n - 1:n, :]


def _cumsum_logf(lf):
    b, length, n = lf.shape
    spec = pl.BlockSpec((1, length, n), lambda i: (i, 0, 0))
    return pl.pallas_call(
        _cumsum_kernel,
        out_shape=jax.ShapeDtypeStruct(lf.shape, F32),
        grid=(b,),
        in_specs=[spec],
        out_specs=spec,
        compiler_params=_cparams(("parallel",)),
        name="cumsum_logf",
    )(lf)


QB = 256
PAIR = 2
ATTN_GROUPS = 2


def _attn_kernel(q_ref, k_ref, v_ref, ct_ref, o_ref, *, n_full, tail):
    step_id = pl.program_id(1)
    lane_head = lax.broadcasted_iota(jnp.int32, (1, LANES), 1) // HEAD_DIM
    one = jnp.ones((), BF16)
    groups = range(ATTN_GROUPS)
    gl = lambda gi: slice(gi * LANES, (gi + 1) * LANES)

    def attend(q_rows, rows, n_off, diag_rows, diag_lo):
        qs = []
        for gi in groups:
            qt = q_ref[0, rows, gl(gi)]
            qs.append(jnp.concatenate([jnp.where(lane_head == hh, qt, jnp.zeros_like(qt)) for hh in range(PAIR)],
                                      axis=0))

        def step(kv, c0, carry, masked):
            s_all = [_dot_nt(qs[gi], kv[gi][0]) for gi in groups]
            soft = []
            for gi in groups:
                for hh in range(PAIR):
                    m, _ = carry[gi * PAIR + hh]
                    head = (step_id * ATTN_GROUPS + gi) * PAIR + hh
                    s = s_all[gi][hh * q_rows:(hh + 1) * q_rows] - ct_ref[0, head, :, pl.ds(c0, QB)]
                    if masked:
                        ri = lax.broadcasted_iota(jnp.int32, (q_rows, QB), 0)
                        ci = lax.broadcasted_iota(jnp.int32, (q_rows, QB), 1)
                        s = jnp.where(ci <= ri, s, NEG_INF)
                    m_new = jnp.maximum(m, jnp.max(s, axis=-1, keepdims=True))
                    soft.append((m_new, jnp.exp2(m - m_new), jnp.exp2(s - m_new).astype(BF16)))
            new = []
            for gi in groups:
                for hh in range(PAIR):
                    m_new, alpha, p = soft[gi * PAIR + hh]
                    pv = _dot(p, jnp.where(lane_head == hh, kv[gi][1], one))
                    new.append((m_new, alpha * carry[gi * PAIR + hh][1] + pv))
            return tuple(new)

        def body(j, carry):
            c0 = pl.multiple_of(j * QB, QB)
            kv = [(k_ref[0, pl.ds(c0, QB), gl(gi)], v_ref[0, pl.ds(c0, QB), gl(gi)]) for gi in groups]
            return step(kv, c0, carry, False)

        init = tuple((jnp.full((q_rows, 1), NEG_INF, F32), jnp.zeros((q_rows, LANES), F32))
                     for _ in range(ATTN_GROUPS * PAIR))
        carry = step(diag_rows, diag_lo, lax.fori_loop(0, n_off, body, init), True)
        for gi in groups:
            outs = [acc / pltpu.roll(acc, HEAD_DIM, axis=1) for _, acc in carry[gi * PAIR:(gi + 1) * PAIR]]
            o_ref[0, rows, gl(gi)] = jnp.where(lane_head == 0, outs[0], outs[1]).astype(o_ref.dtype)

    def qblock(i, _):
        r0 = pl.multiple_of(i * QB, QB)
        rows = pl.ds(r0, QB)
        attend(QB, rows, i, [(k_ref[0, rows, gl(gi)], v_ref[0, rows, gl(gi)]) for gi in groups], r0)
        return 0

    lax.fori_loop(0, n_full, qblock, 0)
    if tail:
        lo = n_full * QB
        length = lo + tail
        rows = slice(lo, length)
        ri = lax.broadcasted_iota(jnp.int32, (tail, length), 0)
        ci = lax.broadcasted_iota(jnp.int32, (tail, length), 1)
        visible = ci <= ri + lo
        for gi in groups:
            qt = q_ref[0, rows, gl(gi)]
            qs = jnp.concatenate([jnp.where(lane_head == hh, qt, jnp.zeros_like(qt)) for hh in range(PAIR)], axis=0)
            s_all = _dot_nt(qs, k_ref[0, :, gl(gi)])
            vt = v_ref[0, :, gl(gi)]
            outs = []
            for hh in range(PAIR):
                head = (step_id * ATTN_GROUPS + gi) * PAIR + hh
                s = s_all[hh * tail:(hh + 1) * tail] - ct_ref[0, head, :, 0:length]
                s = jnp.where(visible, s, NEG_INF)
                p = jnp.exp2(s - jnp.max(s, axis=-1, keepdims=True)).astype(BF16)
                acc = _dot(p, jnp.where(lane_head == hh, vt, one))
                outs.append(acc / pltpu.roll(acc, HEAD_DIM, axis=1))
            o_ref[0, rows, gl(gi)] = jnp.where(lane_head == 0, outs[0], outs[1]).astype(o_ref.dtype)


def _fox_prompt(qb, kb, vb, c_t):
    b, length, _ = qb.shape
    n_full, tail = divmod(length, QB)
    assert tail % 16 == 0 and c_t.shape[3] == (n_full + (1 if tail else 0)) * QB
    qkv = pl.BlockSpec((1, length, LANES * ATTN_GROUPS), lambda i, g: (i, 0, g))
    return pl.pallas_call(
        functools.partial(_attn_kernel, n_full=n_full, tail=tail),
        out_shape=jax.ShapeDtypeStruct(qb.shape, BF16),
        grid=(b, HEADS // (PAIR * ATTN_GROUPS)),
        in_specs=[qkv, qkv, qkv, pl.BlockSpec((1, HEADS, 1, c_t.shape[3]), lambda i, g: (i, 0, 0, 0))],
        out_specs=qkv,
        compiler_params=_cparams(("parallel", "arbitrary")),
        name="fox_prompt",
    )(qb, kb, vb, c_t)


def _head_sum_matrix():
    r = lax.broadcasted_iota(jnp.int32, (WIDTH, WIDTH), 0) // HEAD_DIM
    c = lax.broadcasted_iota(jnp.int32, (WIDTH, WIDTH), 1) // HEAD_DIM
    return (r == c).astype(BF16)


def _rwkv_pointwise(zs, p):
    (w0, w2, a0, a2, g2, k_k, k_a, r_k) = p
    r = zs[:, 0:WIDTH]
    k = zs[:, WIDTH:2 * WIDTH]
    v = zs[:, 2 * WIDTH:3 * WIDTH]
    zw = zs[:, ZW_OFF:ZW_OFF + LANES]
    za = zs[:, ZA_OFF:ZA_OFF + LANES]
    zg = zs[:, ZG_OFF:ZG_OFF + ZG_COLS]
    w_log = _log_sigmoid(w0 + _dot(jnp.tanh(zw).astype(BF16), w2)) - 0.5
    logdec = -jnp.exp(w_log)
    a = _sigmoid(a0 + _dot(za.astype(BF16), a2))
    g = _dot(_sigmoid(zg).astype(BF16), g2)
    hs = _head_sum_matrix()
    kk = k * k_k
    norm = jnp.sqrt(_split_dot(hs, kk * kk, 2, left=False))
    kk = kk / jnp.maximum(norm, 1e-12)
    km = k * (1.0 + (a - 1.0) * k_a)
    bonus = _split_dot(hs, r * km * r_k, 2, left=False) * v
    return r, km, v, kk, a, logdec, g, bonus


def _group_norm_out(y, ln_w, ln_b, bonus, g):
    mean = jnp.mean(y, axis=-1, keepdims=True)
    var = jnp.mean(jnp.square(y - mean), axis=-1, keepdims=True)
    yn = (y - mean) * lax.rsqrt(var + GN_EPS) * ln_w + ln_b
    return (yn + bonus) * g


RCHUNK = 128


def _rwkv_chunk_kernel(zr_ref, zp_ref, sh_ref, mu_ref, w0_ref, w2_ref, a0_ref, a2_ref, g2_ref, kk_ref, ka_ref,
                       rk_ref, lnw_ref, lnb_ref, out_ref, wkv_ref, s_ref, *, length):
    ci = pl.program_id(1)
    n = RCHUNK

    @pl.when(ci == 0)
    def _():
        s_ref[...] = jnp.zeros_like(s_ref)

    zr = zr_ref[0]
    row = lax.broadcasted_iota(jnp.int32, (n, 1), 0)
    prev = jnp.where(ci == 0, sh_ref[0], zp_ref[0, 7:8, :])
    zprev = jnp.where(row == 0, prev, pltpu.roll(zr, 1, axis=0))
    valid = (ci * n + row) < length
    zs = jnp.where(valid, zr + (zprev - zr) * mu_ref[...], 0.0)
    params = (w0_ref[...], w2_ref[...], a0_ref[...], a2_ref[...], g2_ref[...], kk_ref[...], ka_ref[...],
              rk_ref[...])
    r, km, v, kk, a, logdec, g, bonus = _rwkv_pointwise(zs, params)
    logdec = jnp.where(valid, logdec, 0.0)

    ri = lax.broadcasted_iota(jnp.int32, (n, n), 0)
    cj = lax.broadcasted_iota(jnp.int32, (n, n), 1)
    incl = cj <= ri
    strict = cj < ri
    eye = (ri == cj).astype(F32)
    eye_h = eye[:HEAD_DIM, :HEAD_DIM]
    logp = _split_dot(incl.astype(BF16), logdec, 3)
    logp_c = logp[n - 1:n, :]
    e_pos = jnp.exp(logp)
    e_neg = jnp.exp(-logp)
    e_end = jnp.exp(logp_c - logp)
    rt_all = r * e_pos
    pc_all = jnp.exp(logp_c)
    heads = range(HEADS)
    sls = [slice(h * HEAD_DIM, (h + 1) * HEAD_DIM) for h in heads]
    cut = lambda x: [x[:, sl].astype(BF16) for sl in sls]
    rt, kt, at, bt = cut(rt_all), cut(km * e_neg), cut(-kk * jnp.exp(logp - logdec)), cut(kk * a * e_neg)
    bh, kh, vh = cut(kk * a * e_end), cut(km * e_end), cut(v)

    m_all = [_dot_nt(jnp.concatenate([at[h], rt[h]], axis=0), jnp.concatenate([bt[h], kt[h]], axis=0)) for h in heads]
    a_ab = [jnp.where(strict, m[:n, :n], 0.0) for m in m_all]
    a_ak = [jnp.where(strict, m[:n, n:], 0.0).astype(BF16) for m in m_all]
    m_rb = [jnp.where(incl, m[n:, :n], 0.0).astype(BF16) for m in m_all]
    m_rk = [jnp.where(incl, m[n:, n:], 0.0).astype(BF16) for m in m_all]
    t_inv = [eye for _ in heads]
    pw = a_ab
    span = 1
    while span < n:
        pt = [_dot(pw[h].astype(BF16), jnp.concatenate([pw[h], t_inv[h]], axis=1).astype(BF16)) for h in heads]
        pw = [x[:, :n] for x in pt]
        t_inv = [t_inv[h] + pt[h][:, n:] for h in heads]
        span *= 2
    t_bf = [x.astype(BF16) for x in t_inv]
    w_m = [_dot(t_bf[h], at[h]) for h in heads]
    akv = [_dot(a_ak[h], vh[h]).astype(BF16) for h in heads]
    u0 = [_dot(t_bf[h], akv[h]) for h in heads]
    w_bf = [x.astype(BF16) for x in w_m]
    u_bf = [x.astype(BF16) for x in u0]
    qp = [rt_all[:, sls[h]] + _dot(m_rb[h], w_bf[h]) for h in heads]
    y0 = [_dot(m_rb[h], u_bf[h]) + _dot(m_rk[h], vh[h]) for h in heads]
    gt = [eye_h * pc_all[:, sls[h]] + _dot_tn(w_m[h], bh[h]) for h in heads]
    et = [_dot_tn(u0[h], bh[h]) + _dot_tn(v[:, sls[h]], kh[h]) for h in heads]
    for h in heads:
        sl = sls[h]
        s_h = s_ref[h].astype(BF16)
        y = _dot_nt(qp[h].astype(BF16), s_h) + y0[h]
        s_ref[h] = _dot(s_h, gt[h].astype(BF16)) + et[h]
        out_ref[0, :, sl] = _group_norm_out(y, lnw_ref[:, sl], lnb_ref[:, sl], bonus[:, sl],
                                            g[:, sl]).astype(out_ref.dtype)

    @pl.when(ci == pl.num_programs(1) - 1)
    def _():
        wkv_ref[0] = s_ref[...]


def _rwkv_prompt(zr, shift0, rp):
    b, length, _ = zr.shape
    n = RCHUNK
    nchunk = pl.cdiv(length, n)
    vec = _full((1, WIDTH))
    return pl.pallas_call(
        functools.partial(_rwkv_chunk_kernel, length=length),
        out_shape=[jax.ShapeDtypeStruct((b, length, WIDTH), BF16),
                   jax.ShapeDtypeStruct((b, HEADS, HEAD_DIM, HEAD_DIM), F32)],
        grid=(b, nchunk),
        in_specs=[pl.BlockSpec((1, n, R_COLS), lambda i, c: (i, c, 0)),
                  pl.BlockSpec((1, 8, R_COLS), lambda i, c: (i, jnp.maximum(c * (n // 8) - 1, 0), 0)),
                  pl.BlockSpec((1, 1, R_COLS), lambda i, c: (i, 0, 0)),
                  _full((1, R_COLS)), vec, _full((LANES, WIDTH)), vec, _full((LANES, WIDTH)),
                  _full((ZG_COLS, WIDTH)), vec, vec, vec, vec, vec],
        out_specs=[pl.BlockSpec((1, n, WIDTH), lambda i, c: (i, c, 0)),
                   pl.BlockSpec((1, HEADS, HEAD_DIM, HEAD_DIM), lambda i, c: (i, 0, 0, 0))],
        scratch_shapes=[pltpu.VMEM((HEADS, HEAD_DIM, HEAD_DIM), F32)],
        compiler_params=_cparams(("parallel", "arbitrary")),
        name="rwkv_prompt",
    )(zr, zr, shift0, rp["mu"], rp["w0"], rp["w2"], rp["a0"], rp["a2"], rp["g2"], rp["k_k"], rp["k_a"],
      rp["r_k"], rp["ln_w"], rp["ln_b"])


DEC_ROWS = 8


def _rwkv_decode_kernel(zr_ref, sh_ref, st_ref, mu_ref, w0_ref, w2_ref, a0_ref, a2_ref, g2_ref, kk_ref, ka_ref,
                        rk_ref, lnw_ref, lnb_ref, out_ref, ns_ref, vec_ref):
    zr = zr_ref[...]
    zs = zr + (sh_ref[...] - zr) * mu_ref[...]
    params = (w0_ref[...], w2_ref[...], a0_ref[...], a2_ref[...], g2_ref[...], kk_ref[...], ka_ref[...],
              rk_ref[...])
    r, km, v, kk, a, logdec, g, bonus = _rwkv_pointwise(zs, params)
    for idx, x in enumerate((r, km, v, kk, kk * a, jnp.exp(logdec), g, bonus)):
        for s in range(DEC_ROWS):
            vec_ref[idx, s] = x[s:s + 1, :]
    ri = lax.broadcasted_iota(jnp.int32, (HEAD_DIM, HEAD_DIM), 0)
    cj = lax.broadcasted_iota(jnp.int32, (HEAD_DIM, HEAD_DIM), 1)
    eye = ri == cj

    def per_row(s, _):
        for h in range(HEADS):
            sl = slice(h * HEAD_DIM, (h + 1) * HEAD_DIM)
            r_h, k_h, v_h, kk_h, ka_h, w_h, g_h, bo_h = (vec_ref[i, s, :, sl] for i in range(8))
            st = st_ref[s, h]
            sa = jnp.sum(st * (-kk_h), axis=-1, keepdims=True)
            v_col = jnp.sum(jnp.where(eye, v_h, 0.0), axis=-1, keepdims=True)
            st = st * w_h + sa * ka_h + v_col * k_h
            ns_ref[s, h] = st
            y_col = jnp.sum(st * r_h, axis=-1, keepdims=True)
            y = jnp.sum(jnp.where(eye, y_col, 0.0), axis=0, keepdims=True)
            out_ref[s, :, sl] = _group_norm_out(y, lnw_ref[:, sl], lnb_ref[:, sl], bo_h, g_h)
        return 0

    lax.fori_loop(0, DEC_ROWS, per_row, 0)


def _rwkv_decode(zr, shift, state, rp):
    n = zr.shape[0]
    vec = _full((1, WIDTH))
    rows = lambda c: pl.BlockSpec((DEC_ROWS, c), lambda i: (i, 0))
    st = pl.BlockSpec((DEC_ROWS, HEADS, HEAD_DIM, HEAD_DIM), lambda i: (i, 0, 0, 0))
    return pl.pallas_call(
        _rwkv_decode_kernel,
        out_shape=[jax.ShapeDtypeStruct((n, 1, WIDTH), F32), jax.ShapeDtypeStruct(state.shape, F32)],
        grid=(n // DEC_ROWS,),
        in_specs=[rows(R_COLS), rows(R_COLS), st, _full((1, R_COLS)), vec, _full((LANES, WIDTH)), vec,
                  _full((LANES, WIDTH)), _full((ZG_COLS, WIDTH)), vec, vec, vec, vec, vec],
        out_specs=[pl.BlockSpec((DEC_ROWS, 1, WIDTH), lambda i: (i, 0, 0)), st],
        scratch_shapes=[pltpu.VMEM((8, DEC_ROWS, 1, WIDTH), F32)],
        compiler_params=_cparams(("parallel",)),
        name="rwkv_decode",
    )(zr, shift, state, rp["mu"], rp["w0"], rp["w2"], rp["a0"], rp["a2"], rp["g2"], rp["k_k"], rp["k_a"],
      rp["r_k"], rp["ln_w"], rp["ln_b"])


SUFFIX_PAGES = 512
DEC_PAGES = 32


def _page_suffix_kernel(lf_ref, et_ref):
    pages, heads, plen = lf_ref.shape
    x = lf_ref[...].reshape(pages * heads, plen)
    u = lax.broadcasted_iota(jnp.int32, (plen, plen), 0)
    t = lax.broadcasted_iota(jnp.int32, (plen, plen), 1)
    after = _split_dot((u > t).astype(BF16), x, 3, left=False)
    total = _split_dot(jnp.ones((plen, plen), BF16), x, 3, left=False)
    et_ref[:, 0] = (after * LOG2E).reshape(pages, heads, plen)
    et_ref[:, 1] = (total * LOG2E).reshape(pages, heads, plen)


def _page_suffix(lft):
    n_pool, heads, plen = lft.shape
    return pl.pallas_call(
        _page_suffix_kernel,
        out_shape=jax.ShapeDtypeStruct((n_pool, 2, heads, plen), F32),
        grid=(pl.cdiv(n_pool, SUFFIX_PAGES),),
        in_specs=[pl.BlockSpec((SUFFIX_PAGES, heads, plen), lambda i: (i, 0, 0))],
        out_specs=pl.BlockSpec((SUFFIX_PAGES, 2, heads, plen), lambda i: (i, 0, 0, 0)),
        compiler_params=_cparams(("parallel",)),
        name="page_suffix",
    )(lft)


def _fox_decode_kernel(pt_ref, q_ref, kn_ref, vn_ref, lfn_ref, *refs):
    del pt_ref
    k_refs, v_refs, et_refs = (refs[i * DEC_PAGES:(i + 1) * DEC_PAGES] for i in range(3))
    o_ref, m_ref, l_ref, acc_ref, carry_ref = refs[3 * DEC_PAGES:]
    g = pl.program_id(1)
    plen = k_refs[0].shape[4]

    @pl.when(g == 0)
    def _():
        m_ref[...] = jnp.full_like(m_ref, NEG_INF)
        l_ref[...] = jnp.zeros_like(l_ref)
        acc_ref[...] = jnp.zeros_like(acc_ref)
        carry_ref[...] = lfn_ref[0] * LOG2E

    for h in range(HEADS):
        row = slice(h, h + 1)
        q_b = jnp.broadcast_to(q_ref[0, h], (HEAD_DIM, plen))
        carry = carry_ref[row, :]
        scores = []
        for i in range(DEC_PAGES):
            s = jnp.sum(k_refs[i][0, 0, h] * q_b, axis=0, keepdims=True)
            scores.append(s + et_refs[i][0, 0, row, :] + carry)
            carry = carry + et_refs[i][0, 1, row, 0:1]
        carry_ref[row, :] = carry
        top = scores[0]
        for s in scores[1:]:
            top = jnp.maximum(top, s)
        m_old = m_ref[row, :]
        m_new = jnp.maximum(m_old, jnp.max(top, axis=-1, keepdims=True))
        alpha = jnp.exp2(m_old - m_new)
        m_ref[row, :] = m_new
        psum = None
        pv = None
        for i in range(DEC_PAGES):
            pr = jnp.exp2(scores[i] - m_new)
            part = v_refs[i][0, 0, h] * pr
            psum = pr if psum is None else psum + pr
            pv = part if pv is None else pv + part
        l_ref[row, :] = alpha * l_ref[row, :] + jnp.sum(psum, axis=-1, keepdims=True)
        acc_ref[h] = alpha * acc_ref[h] + pv

    @pl.when(g == pl.num_programs(1) - 1)
    def _():
        for h in range(HEADS):
            row = slice(h, h + 1)
            s_new = jnp.sum(q_ref[0, h] * kn_ref[0, h], axis=0, keepdims=True)
            m_fin = jnp.maximum(m_ref[row, :], s_new)
            a_fin = jnp.exp2(m_ref[row, :] - m_fin)
            p_new = jnp.exp2(s_new - m_fin)
            num = a_fin * jnp.sum(acc_ref[h], axis=-1, keepdims=True) + p_new * vn_ref[0, h]
            o_ref[0, h] = num / (a_fin * l_ref[row, :] + p_new)


def _fox_decode(page_table, q, k_new, v_new, lf_new, cache_kt, cache_vt, et):
    n, n_pages = page_table.shape
    page = cache_kt.shape[4]
    assert n_pages % DEC_PAGES == 0
    tok = pl.BlockSpec((1, HEADS, HEAD_DIM, 1), lambda s, g, pt: (s, 0, 0, 0))

    def paged(shape, i, lead):
        idx = lambda s, g, pt: lead + (pt[s, n_pages - 1 - (g * DEC_PAGES + i)],) + (0,) * (len(shape) - len(lead) - 1)
        return pl.BlockSpec(shape, idx)

    slots = range(DEC_PAGES)
    kv_shape = (1, 1, HEADS, HEAD_DIM, page)
    grid_spec = pltpu.PrefetchScalarGridSpec(
        num_scalar_prefetch=1,
        grid=(n, n_pages // DEC_PAGES),
        in_specs=[tok, tok, tok, pl.BlockSpec((1, HEADS, 1), lambda s, g, pt: (s, 0, 0))]
        + [paged(kv_shape, i, (0,)) for i in slots] + [paged(kv_shape, i, (0,)) for i in slots]
        + [paged((1, 2, HEADS, page), i, ()) for i in slots],
        out_specs=tok,
        scratch_shapes=[pltpu.VMEM((HEADS, 1), F32), pltpu.VMEM((HEADS, 1), F32),
                        pltpu.VMEM((HEADS, HEAD_DIM, page), F32), pltpu.VMEM((HEADS, 1), F32)],
    )
    return pl.pallas_call(
        _fox_decode_kernel,
        out_shape=jax.ShapeDtypeStruct((n, HEADS, HEAD_DIM, 1), F32),
        grid_spec=grid_spec,
        compiler_params=_cparams(("parallel", "arbitrary")),
        name="fox_decode",
    )(page_table, q, k_new, v_new, lf_new, *([cache_kt] * DEC_PAGES), *([cache_vt] * DEC_PAGES), *([et] * DEC_PAGES))


def _outproj_kernel(oa_ref, orw_ref, sg_ref, x_ref, woa_ref, wor_ref, wout_ref, nf_ref, wrt_ref, brc_ref, *refs,
                    n_valid):
    x1_ref, h2t_ref, gate_ref, gate3_ref = refs[-4:]
    i = pl.program_id(0)
    d = x_ref.shape[1]
    n_exp = wrt_ref.shape[0]

    @pl.when(i < n_valid)
    def _():
        a = _dot(oa_ref[...], woa_ref[...])
        b = _dot(orw_ref[...], wor_ref[...])
        merged = sg_ref[:, :d].astype(F32) * a + sg_ref[:, d:].astype(F32) * b
        x1 = x_ref[...] + _dot(merged.astype(BF16), wout_ref[...])
        x1_ref[...] = x1
        h2 = _rms(x1, nf_ref[...])
        h2t_ref[...] = h2.T.astype(BF16)
        work = _dot_nt(wrt_ref[...], h2, HIGHEST) + brc_ref[...]
        row = lax.broadcasted_iota(jnp.int32, work.shape, 0)
        top = None
        denom = jnp.zeros((1, work.shape[1]), F32)
        picks = []
        for _ in range(TOP_K):
            mx = jnp.max(work, axis=0, keepdims=True)
            idx = jnp.min(jnp.where(work == mx, row, n_exp), axis=0, keepdims=True)
            hit = row == idx
            top = mx if top is None else top
            e = jnp.exp(mx - top)
            denom = denom + e
            picks.append((hit, e))
            work = jnp.where(hit, -jnp.inf, work)
        gate = jnp.zeros(work.shape, F32)
        for hit, e in picks:
            gate = jnp.where(hit, e / denom, gate)
        gate_ref[...] = gate
        for ex in range(n_exp):
            gate3_ref[ex] = gate[ex:ex + 1, :]

    @pl.when(i >= n_valid)
    def _():
        x1_ref[...] = jnp.zeros_like(x1_ref)
        h2t_ref[...] = jnp.zeros_like(h2t_ref)
        gate_ref[...] = jnp.zeros_like(gate_ref)
        gate3_ref[...] = jnp.zeros_like(gate3_ref)


def _outproj(o_attn, o_rwkv, sg, x, w_oa, w_or, w_out, norm_ffn, w_router_t, b_router_col, tm, m_pad, into=None,
             base=0):
    m, d = x.shape
    n_exp = w_router_t.shape[0]
    n_valid = m // tm
    assert m % tm == 0 and m_pad % tm == 0 and base % tm == 0
    last = n_valid - 1
    off = base // tm
    row = lambda n: pl.BlockSpec((tm, n), lambda i: (jnp.minimum(i, last), 0))
    operands = [o_attn, o_rwkv, sg, x, w_oa, w_or, w_out, norm_ffn, w_router_t, b_router_col]
    in_specs = [row(WIDTH), row(WIDTH), row(2 * d), row(d), _full((WIDTH, d)), _full((WIDTH, d)), _full((d, d)),
                _full((1, d)), _full((n_exp, d)), _full((n_exp, 1))]
    if into is None:
        out_shape = [jax.ShapeDtypeStruct((m_pad, d), F32), jax.ShapeDtypeStruct((d, m_pad), BF16),
                     jax.ShapeDtypeStruct((n_exp, m_pad), F32), jax.ShapeDtypeStruct((n_exp, 1, m_pad), F32)]
        aliases = {}
        steps = m_pad // tm
    else:
        out_shape = [jax.ShapeDtypeStruct(a.shape, a.dtype) for a in into]
        aliases = {len(operands) + k: k for k in range(len(into))}
        operands = operands + list(into)
        in_specs = in_specs + [pl.BlockSpec(memory_space=pl.ANY)] * len(into)
        steps = n_valid
    return pl.pallas_call(
        functools.partial(_outproj_kernel, n_valid=n_valid),
        out_shape=out_shape,
        grid=(steps,),
        in_specs=in_specs,
        out_specs=[pl.BlockSpec((tm, d), lambda i: (i + off, 0)), pl.BlockSpec((d, tm), lambda i: (0, i + off)),
                   pl.BlockSpec((n_exp, tm), lambda i: (0, i + off)),
                   pl.BlockSpec((n_exp, 1, tm), lambda i: (0, 0, i + off))],
        input_output_aliases=aliases,
        compiler_params=_cparams(("parallel",)),
        name="outproj_router",
    )(*operands)


MOE_TOKENS = 1792
MOE_SLOTS = 256
MOE_FF_CHUNK = 256


def _moe_kernel(h2t_ref, gate_ref, gate3_ref, wu_ref, bu_ref, wd_ref, bd_ref, o_ref, acc_ref, rank_ref, xg_ref,
                sg_ref, dn_ref):
    e = pl.program_id(1)
    last = pl.num_programs(1) - 1
    d, tt = h2t_ref.shape
    n_exp = gate_ref.shape[0]
    d_ff = wd_ref.shape[2]
    reps = MOE_SLOTS // LANES
    cols = lambda ref, lo, n: jnp.concatenate([ref[0, lo:lo + n, :]] * reps, axis=1)

    def gather(ex, blk):
        g_row = gate3_ref[ex]
        slot_of = jnp.where(g_row > 0.0, rank_ref[ex], -1.0)
        slot = (blk * MOE_SLOTS + lax.broadcasted_iota(jnp.int32, (MOE_SLOTS, 1), 0)).astype(F32)
        sel = slot_of == slot
        sel_one = jnp.where(sel, 1.0, 0.0).astype(BF16)
        return _dot_nt(h2t_ref[...], sel_one).astype(BF16), jnp.where(sel, g_row, 0.0).astype(BF16)

    def ffn(xg):
        def up(lo):
            return (_dot(wu_ref[0, lo:lo + MOE_FF_CHUNK, :], xg),
                    _dot(wu_ref[0, d_ff + lo:d_ff + lo + MOE_FF_CHUNK, :], xg))

        dn = None
        starts = list(range(0, d_ff, MOE_FF_CHUNK))
        ahead = up(starts[0])
        for ci, lo in enumerate(starts):
            x_glu, x_lin = ahead
            if ci + 1 < len(starts):
                ahead = up(starts[ci + 1])
            x_glu = jnp.minimum(x_glu + cols(bu_ref, lo, MOE_FF_CHUNK), SWIGLU_LIMIT)
            x_lin = jnp.clip(x_lin + cols(bu_ref, d_ff + lo, MOE_FF_CHUNK), -SWIGLU_LIMIT, SWIGLU_LIMIT)
            act = (x_glu * _sigmoid(SWIGLU_ALPHA * x_glu) * (x_lin + 1.0)).astype(BF16)
            part = _dot(wd_ref[0, :, lo:lo + MOE_FF_CHUNK], act)
            dn = part if dn is None else dn + part
        return jnp.concatenate([(dn[lo:lo + MOE_FF_CHUNK] + cols(bd_ref, lo, MOE_FF_CHUNK)).astype(BF16)
                                for lo in range(0, d, MOE_FF_CHUNK)], axis=0)

    def scatter(dn, sel_gate):
        for lo in range(0, d, MOE_FF_CHUNK):
            rows = slice(lo, lo + MOE_FF_CHUNK)
            acc_ref[rows, :] += _dot(dn[rows], sel_gate)

    @pl.when(e == 0)
    def _():
        acc_ref[...] = jnp.zeros_like(acc_ref)
        tr = lax.broadcasted_iota(jnp.int32, (tt, tt), 0)
        tc = lax.broadcasted_iota(jnp.int32, (tt, tt), 1)
        hit_all = jnp.where(gate_ref[...] > 0.0, 1.0, 0.0).astype(BF16)
        rank = _dot(hit_all, (tr < tc).astype(BF16))
        for ex in range(n_exp):
            rank_ref[ex] = rank[ex:ex + 1, :]
        xg0, sg0 = gather(0, 0)
        xg_ref[0] = xg0
        sg_ref[0] = sg0
        dn_ref[...] = jnp.zeros_like(dn_ref)
        sg_ref[2] = jnp.zeros_like(sg0)

    scatter(dn_ref[...], sg_ref[(e + 2) % 3])
    dn_new = ffn(xg_ref[e % 2])
    xg_next, sg_next = gather(jnp.minimum(e + 1, last), 0)
    xg_ref[(e + 1) % 2] = xg_next
    sg_ref[(e + 1) % 3] = sg_next
    dn_ref[...] = dn_new

    def extra_pass(blk, _):
        xg, sel_gate = gather(e, blk)
        scatter(ffn(xg), sel_gate)
        return 0

    count = jnp.sum(jnp.where(gate3_ref[e] > 0.0, 1.0, 0.0)).astype(jnp.int32)
    lax.fori_loop(1, (count + MOE_SLOTS - 1) // MOE_SLOTS, extra_pass, 0)

    @pl.when(e == last)
    def _():
        scatter(dn_ref[...], sg_ref[last % 3])
        o_ref[...] = acc_ref[...].T.astype(o_ref.dtype)


def _moe(h2t, gate, gate3, w_up_t, b_up, w_down_t, b_down, tt):
    d, m_pad = h2t.shape
    n_exp, d_up, _ = w_up_t.shape
    assert m_pad % tt == 0
    return pl.pallas_call(
        _moe_kernel,
        out_shape=jax.ShapeDtypeStruct((m_pad, d), BF16),
        grid=(m_pad // tt, n_exp),
        in_specs=[pl.BlockSpec((d, tt), lambda i, e: (0, i)),
                  pl.BlockSpec((n_exp, tt), lambda i, e: (0, i)),
                  pl.BlockSpec((n_exp, 1, tt), lambda i, e: (0, 0, i)),
                  pl.BlockSpec((1, d_up, d), lambda i, e: (e, 0, 0)),
                  pl.BlockSpec((1, d_up, LANES), lambda i, e: (e, 0, 0)),
                  pl.BlockSpec((1, d, w_down_t.shape[2]), lambda i, e: (e, 0, 0)),
                  pl.BlockSpec((1, d, LANES), lambda i, e: (e, 0, 0))],
        out_specs=pl.BlockSpec((tt, d), lambda i, e: (i, 0)),
        scratch_shapes=[pltpu.VMEM((d, tt), F32), pltpu.VMEM((n_exp, 1, tt), F32),
                        pltpu.VMEM((2, d, MOE_SLOTS), BF16), pltpu.VMEM((3, MOE_SLOTS, tt), BF16),
                        pltpu.VMEM((d, MOE_SLOTS), BF16)],
        compiler_params=_cparams(("parallel", "arbitrary")),
        name="moe_routed",
    )(h2t, gate, gate3, w_up_t, b_up, w_down_t, b_down)


def _final_kernel(x1_ref, moe_ref, g_ref, y_ref):
    y_ref[...] = _rms(x1_ref[...] + moe_ref[...].astype(F32), g_ref[...]).reshape(y_ref.shape)


def _final_norm_prompt(x1, moe, g, b, length, skip, tm):
    d = x1.shape[1]
    seq = length - skip
    assert seq % tm == 0 and skip % 8 == 0
    assert length % 8 == 0
    rows = pl.BlockSpec((pl.Element(tm), pl.Element(d)),
                        lambda i, j: (pl.multiple_of(i * length + skip + j * tm, 8), 0))
    return pl.pallas_call(
        _final_kernel,
        out_shape=jax.ShapeDtypeStruct((b, seq, d), F32),
        grid=(b, seq // tm),
        in_specs=[rows, rows, _full((1, d))],
        out_specs=pl.BlockSpec((1, tm, d), lambda i, j: (i, j, 0)),
        compiler_params=_cparams(("parallel", "parallel")),
        name="final_norm_prompt",
    )(x1, moe, g)


def _final_norm(x1, moe, g, m, tm, base):
    d = x1.shape[1]
    assert base % tm == 0 and m % tm == 0
    off = base // tm
    row = pl.BlockSpec((tm, d), lambda i: (i + off, 0))
    return pl.pallas_call(
        _final_kernel,
        out_shape=jax.ShapeDtypeStruct((m, d), F32),
        grid=(m // tm,),
        in_specs=[row, row, _full((1, d))],
        out_specs=pl.BlockSpec((tm, d), lambda i: (i, 0)),
        compiler_params=_cparams(("parallel",)),
        name="final_norm",
    )(x1, moe, g)


def _pad_cols(a, n):
    return jnp.pad(a, ((0, 0), (0, n - a.shape[1])))


def _pack_rwkv_cols(a):
    o = 3 * WIDTH
    return jnp.concatenate([
        a[:, :o], _pad_cols(a[:, o:o + DECAY_LORA], LANES),
        _pad_cols(a[:, o + DECAY_LORA:o + DECAY_LORA + AAA_LORA], LANES),
        _pad_cols(a[:, o + DECAY_LORA + AAA_LORA:], ZG_COLS)], axis=1)


def _unpack_rwkv_cols(a):
    return jnp.concatenate([a[..., :ZW_OFF], a[..., ZW_OFF:ZW_OFF + DECAY_LORA], a[..., ZA_OFF:ZA_OFF + AAA_LORA],
                            a[..., ZG_OFF:ZG_OFF + GATE_LORA]], axis=-1)


def _pad_rows(a, n):
    return jnp.pad(a, ((0, n - a.shape[0]), (0, 0)))


def kernel(x_prompt, x_sample, cache_k, cache_v, cache_logf, page_table, state_wkv, state_shift, meta_tokens, norm_mix, w_in, b_f, w_oa, rwkv_mu, rwkv_w0, rwkv_w2, rwkv_a0, rwkv_a2, rwkv_g2, rwkv_k_k, rwkv_k_a, rwkv_r_k, rwkv_ln_w, rwkv_ln_b, w_or, w_out, norm_ffn, w_router, b_router, w_up, b_up, w_down, b_down, norm_final):
    b, seq, d = x_prompt.shape
    db = x_sample.shape[0]
    depth = w_in.shape[0]
    assert depth == 1 and x_sample.shape[1] == 1
    length = N_META + seq
    n_exp = w_router.shape[2]
    a_cols = 3 * WIDTH + HEADS
    r_cols = 3 * WIDTH + DECAY_LORA + AAA_LORA + GATE_LORA

    wi = w_in[0]
    w_packed = jnp.concatenate([
        wi[:, :QKV_COLS], _pad_cols(wi[:, QKV_COLS:a_cols], F_COLS),
        _pack_rwkv_cols(wi[:, a_cols:a_cols + r_cols]), wi[:, a_cols + r_cols:]], axis=1).astype(BF16)
    bf_pad = _pad_cols(b_f[0][None, :], F_COLS)
    rp = {
        "mu": _pack_rwkv_cols(rwkv_mu[0][None, :]),
        "w0": rwkv_w0[0][None, :], "a0": rwkv_a0[0][None, :],
        "w2": _pad_rows(rwkv_w2[0], LANES).astype(BF16), "a2": _pad_rows(rwkv_a2[0], LANES).astype(BF16),
        "g2": _pad_rows(rwkv_g2[0], ZG_COLS).astype(BF16),
        "k_k": rwkv_k_k[0][None, :], "k_a": rwkv_k_a[0][None, :], "r_k": rwkv_r_k[0].reshape(1, WIDTH),
        "ln_w": rwkv_ln_w[0][None, :], "ln_b": rwkv_ln_b[0][None, :],
    }
    g_mix = norm_mix[0][None, :]
    g_ffn = norm_ffn[0][None, :]
    g_fin = norm_final[None, :]
    woa, wor, wout = w_oa[0].astype(BF16), w_or[0].astype(BF16), w_out[0].astype(BF16)
    wr_t = jnp.transpose(w_router[0])
    br_col = b_router[0][:, None]
    wu_t = jnp.transpose(w_up[0], (0, 2, 1)).astype(BF16)
    wd_t = jnp.transpose(w_down[0], (0, 2, 1)).astype(BF16)
    bu = jnp.broadcast_to(b_up[0][:, :, None], b_up.shape[1:] + (LANES,))
    bd = jnp.broadcast_to(b_down[0][:, :, None], b_down.shape[1:] + (LANES,))

    meta = jnp.broadcast_to(meta_tokens.astype(x_prompt.dtype)[None], (b, N_META, d))
    m_p = b * length
    xp = jnp.concatenate([meta, x_prompt], axis=1).reshape(m_p, d)
    qb, kb, vb, k_p, v_p, lf_p, zr_p, sg_p = _inproj(xp, g_mix, w_packed, bf_pad, 512)
    c2 = _cumsum_logf(lf_p.reshape(b, length, F_COLS))
    lpad = pl.cdiv(length, QB) * QB
    c_t = jnp.pad(jnp.transpose(c2[:, :, :HEADS], (0, 2, 1)), ((0, 0), (0, 0), (0, lpad - length)))[:, :, None, :]
    as3 = lambda a: a.reshape(b, length, a.shape[-1])
    o_attn = _fox_prompt(as3(qb), as3(kb), as3(vb), c_t)
    zr3 = as3(zr_p)
    o_rwkv, wkv_p = _rwkv_prompt(zr3, jnp.zeros((b, 1, R_COLS), F32), rp)
    m_pad = pl.cdiv(m_p + db, MOE_TOKENS) * MOE_TOKENS
    routed_p = _outproj(o_attn.reshape(m_p, WIDTH), o_rwkv.reshape(m_p, WIDTH), sg_p, xp, woa, wor, wout, g_ffn, wr_t,
                        br_col, 256, m_pad)

    xs = x_sample.reshape(db, d)
    qs, _, _, k_s, v_s, lf_s, zr_s, sg_s = _inproj(xs, g_mix, w_packed, bf_pad, db)
    col = lambda a: a.reshape(db, HEADS, HEAD_DIM, 1)
    to_lanes = lambda c: jnp.transpose(c, (0, 1, 3, 4, 2))
    et_page = _page_suffix(jnp.transpose(cache_logf[0], (0, 2, 1)))
    o_attn_s = _fox_decode(page_table, col(qs.astype(F32)), col(k_s), col(v_s), lf_s[:, :HEADS, None],
                           to_lanes(cache_k), to_lanes(cache_v), et_page)
    o_rwkv_s, wkv_s = _rwkv_decode(zr_s, _pack_rwkv_cols(state_shift[0, :, 0, :]), state_wkv[0], rp)
    x1, h2t, gate, gate3 = _outproj(o_attn_s.reshape(db, WIDTH).astype(BF16), o_rwkv_s.reshape(db, WIDTH).astype(BF16),
                                    sg_s, xs, woa, wor, wout, g_ffn, wr_t, br_col, db, m_pad, into=routed_p, base=m_p)

    moe = _moe(h2t, gate, gate3, wu_t, bu, wd_t, bd, MOE_TOKENS)
    y_prompt = _final_norm_prompt(x1, moe, g_fin, b, length, N_META, 512)
    y_s = _final_norm(x1, moe, g_fin, db, db, m_p)

    y_sample = y_s.reshape(db, 1, d)
    hd = lambda a, n, t: a.reshape(1, n, t, HEADS, HEAD_DIM)
    logf_prompt = lf_p[:, :HEADS].reshape(1, b, length, HEADS)
    logf_sample = lf_s[:, :HEADS].reshape(1, db, 1, HEADS)
    shift_prompt = _unpack_rwkv_cols(zr3[:, length - 1:length, :])[None]
    shift_sample = _unpack_rwkv_cols(zr_s)[None, :, None, :]
    return (y_prompt, y_sample, hd(k_p, b, length), hd(v_p, b, length), logf_prompt, hd(k_s, db, 1),
            hd(v_s, db, 1), logf_sample, wkv_p[None], shift_prompt, wkv_s[None], shift_sample)
```

```python
import functools
import math

import jax
import jax.numpy as jnp
from jax import lax
from jax.experimental import pallas as pl
from jax.experimental.pallas import tpu as pltpu

F32 = jnp.float32
BF16 = jnp.bfloat16
HIGHEST = lax.Precision.HIGHEST

N_META = 16
HEADS = 8
HEAD_DIM = 64
WIDTH = HEADS * HEAD_DIM
DECAY_LORA = 64
AAA_LORA = 64
GATE_LORA = 160
ATTN_SCALE = 1.0 / math.sqrt(HEAD_DIM)
LOG2E = math.log2(math.e)
NEG_INF = -1e30
GN_EPS = 64e-5
NORM_EPS = 1e-5
TOP_K = 4
SWIGLU_ALPHA = 1.702
SWIGLU_LIMIT = 7.0

LANES = 128
QKV_COLS = 3 * WIDTH
F_OFF = QKV_COLS
F_COLS = LANES
R_OFF = F_OFF + F_COLS
ZW_OFF = 3 * WIDTH
ZA_OFF = ZW_OFF + LANES
ZG_OFF = ZA_OFF + LANES
ZG_COLS = 2 * LANES
R_COLS = ZG_OFF + ZG_COLS
G_OFF = R_OFF + R_COLS
VMEM_LIMIT = 56 * 1024 * 1024


def _cparams(sem, flags=None):
    return pltpu.CompilerParams(dimension_semantics=sem, vmem_limit_bytes=VMEM_LIMIT, flags=flags)


def _full(shape):
    n = len(shape)
    return pl.BlockSpec(shape, lambda *_: (0,) * n)


def _log_sigmoid(x):
    return jnp.minimum(x, 0.0) - jnp.log1p(jnp.exp(-jnp.abs(x)))


def _sigmoid(x):
    return 1.0 / (1.0 + jnp.exp(-x))


def _rms(x, g):
    return x * lax.rsqrt(jnp.mean(x * x, axis=-1, keepdims=True) + NORM_EPS) * g


def _dot(a, b, prec=None):
    return jnp.dot(a, b, preferred_element_type=F32, precision=prec)


def _dot_nt(a, b, prec=None):
    return lax.dot_general(a, b, (((1,), (1,)), ((), ())), preferred_element_type=F32, precision=prec)


def _dot_tn(a, b):
    return _dot(a.T.astype(BF16), b)


def _split_dot(m, x, terms, left=True):
    acc = None
    for _ in range(terms):
        hi = x.astype(BF16)
        part = _dot(m, hi) if left else _dot(hi, m)
        acc = part if acc is None else acc + part
        x = x - hi.astype(F32)
    return acc


def _inproj_kernel(x_ref, g_ref, w_ref, bf_ref, qb_ref, kb_ref, vb_ref, k_ref, v_ref, lf_ref, zr_ref, sg_ref):
    h = _rms(x_ref[...], g_ref[...]).astype(BF16)

    def mm(lo, n):
        return _dot(h, w_ref[:, lo:lo + n])

    qb_ref[...] = (mm(0, WIDTH) * (ATTN_SCALE * LOG2E)).astype(BF16)
    k = mm(WIDTH, WIDTH)
    k_ref[...] = k
    kb_ref[...] = k.astype(BF16)
    v = mm(2 * WIDTH, WIDTH)
    v_ref[...] = v
    vb_ref[...] = v.astype(BF16)
    lf_ref[...] = _log_sigmoid(mm(F_OFF, F_COLS) + bf_ref[...])
    for c in range(R_COLS // WIDTH):
        zr_ref[:, c * WIDTH:(c + 1) * WIDTH] = mm(R_OFF + c * WIDTH, WIDTH)
    for c in range(2 * 1024 // WIDTH):
        sg_ref[:, c * WIDTH:(c + 1) * WIDTH] = _sigmoid(mm(G_OFF + c * WIDTH, WIDTH)).astype(BF16)


def _inproj(x, g, w_packed, bf_pad, tm):
    m, d = x.shape
    ncols = w_packed.shape[1]
    row = lambda n: pl.BlockSpec((tm, n), lambda i: (i, 0))
    outs = [
        jax.ShapeDtypeStruct((m, WIDTH), BF16), jax.ShapeDtypeStruct((m, WIDTH), BF16),
        jax.ShapeDtypeStruct((m, WIDTH), BF16), jax.ShapeDtypeStruct((m, WIDTH), F32),
        jax.ShapeDtypeStruct((m, WIDTH), F32), jax.ShapeDtypeStruct((m, F_COLS), F32),
        jax.ShapeDtypeStruct((m, R_COLS), F32), jax.ShapeDtypeStruct((m, 2 * d), BF16),
    ]
    return pl.pallas_call(
        _inproj_kernel,
        out_shape=outs,
        grid=(pl.cdiv(m, tm),),
        in_specs=[row(d), _full((1, d)), _full((d, ncols)), _full((1, F_COLS))],
        out_specs=[row(WIDTH), row(WIDTH), row(WIDTH), row(WIDTH), row(WIDTH), row(F_COLS), row(R_COLS),
                   row(2 * d)],
        compiler_params=_cparams(("parallel",)),
        name="inproj",
    )(x, g, w_packed, bf_pad)


CUM_BLOCK = 256


def _cumsum_kernel(lf_ref, c_ref):
    length = lf_ref.shape[1]
    r = lax.broadcasted_iota(jnp.int32, (CUM_BLOCK, CUM_BLOCK), 0)
    c = lax.broadcasted_iota(jnp.int32, (CUM_BLOCK, CUM_BLOCK), 1)
    tri = (c <= r).astype(F32)
    carry = jnp.zeros((1, lf_ref.shape[2]), F32)
    for lo in range(0, length, CUM_BLOCK):
        n = min(CUM_BLOCK, length - lo)
        blk = _dot(tri[:n, :n], lf_ref[0, lo:lo + n, :], HIGHEST) + carry
        c_ref[0, lo:lo + n, :] = blk * LOG2E
        carry = blk[n - 1:n, :]


def _cumsum_logf(lf):
    b, length, n = lf.shape
    spec = pl.BlockSpec((1, length, n), lambda i: (i, 0, 0))
    return pl.pallas_call(
        _cumsum_kernel,
        out_shape=jax.ShapeDtypeStruct(lf.shape, F32),
        grid=(b,),
        in_specs=[spec],
        out_specs=spec,
        compiler_params=_cparams(("parallel",)),
        name="cumsum_logf",
    )(lf)


QB = 256
PAIR = 2
ATTN_GROUPS = 4


def _attn_kernel(q_ref, k_ref, v_ref, ct_ref, o_ref, *, n_full, tail):
    step_id = pl.program_id(1)
    lane_head = lax.broadcasted_iota(jnp.int32, (1, LANES), 1) // HEAD_DIM
    one = jnp.ones((), BF16)
    groups = range(ATTN_GROUPS)
    gl = lambda gi: slice(gi * LANES, (gi + 1) * LANES)

    def attend(q_rows, rows, n_off, diag_rows, diag_lo):
        qs = []
        for gi in groups:
            qt = q_ref[0, rows, gl(gi)]
            qs.append(jnp.concatenate([jnp.where(lane_head == hh, qt, jnp.zeros_like(qt)) for hh in range(PAIR)],
                                      axis=0))

        def step(kv, c0, carry, masked):
            s_all = [_dot_nt(qs[gi], kv[gi][0]) for gi in groups]
            soft = []
            for gi in groups:
                for hh in range(PAIR):
                    m, _ = carry[gi * PAIR + hh]
                    head = (step_id * ATTN_GROUPS + gi) * PAIR + hh
                    s = s_all[gi][hh * q_rows:(hh + 1) * q_rows] - ct_ref[0, head, :, pl.ds(c0, QB)]
                    if masked:
                        ri = lax.broadcasted_iota(jnp.int32, (q_rows, QB), 0)
                        ci = lax.broadcasted_iota(jnp.int32, (q_rows, QB), 1)
                        s = jnp.where(ci <= ri, s, NEG_INF)
                    m_new = jnp.maximum(m, jnp.max(s, axis=-1, keepdims=True))
                    soft.append((m_new, jnp.exp2(m - m_new), jnp.exp2(s - m_new).astype(BF16)))
            new = []
            for gi in groups:
                for hh in range(PAIR):
                    m_new, alpha, p = soft[gi * PAIR + hh]
                    pv = _dot(p, jnp.where(lane_head == hh, kv[gi][1], one))
                    new.append((m_new, alpha * carry[gi * PAIR + hh][1] + pv))
            return tuple(new)

        def body(j, carry):
            c0 = pl.multiple_of(j * QB, QB)
            kv = [(k_ref[0, pl.ds(c0, QB), gl(gi)], v_ref[0, pl.ds(c0, QB), gl(gi)]) for gi in groups]
            return step(kv, c0, carry, False)

        init = tuple((jnp.full((q_rows, 1), NEG_INF, F32), jnp.zeros((q_rows, LANES), F32))
                     for _ in range(ATTN_GROUPS * PAIR))
        carry = step(diag_rows, diag_lo, lax.fori_loop(0, n_off, body, init), True)
        for gi in groups:
            outs = [acc / pltpu.roll(acc, HEAD_DIM, axis=1) for _, acc in carry[gi * PAIR:(gi + 1) * PAIR]]
            o_ref[0, rows, gl(gi)] = jnp.where(lane_head == 0, outs[0], outs[1]).astype(o_ref.dtype)

    def qblock(i, _):
        r0 = pl.multiple_of(i * QB, QB)
        rows = pl.ds(r0, QB)
        attend(QB, rows, i, [(k_ref[0, rows, gl(gi)], v_ref[0, rows, gl(gi)]) for gi in groups], r0)
        return 0

    lax.fori_loop(0, n_full, qblock, 0)
    if tail:
        lo = n_full * QB
        length = lo + tail
        rows = slice(lo, length)
        ri = lax.broadcasted_iota(jnp.int32, (tail, length), 0)
        ci = lax.broadcasted_iota(jnp.int32, (tail, length), 1)
        visible = ci <= ri + lo
        for gi in groups:
            qt = q_ref[0, rows, gl(gi)]
            qs = jnp.concatenate([jnp.where(lane_head == hh, qt, jnp.zeros_like(qt)) for hh in range(PAIR)], axis=0)
            s_all = _dot_nt(qs, k_ref[0, :, gl(gi)])
            vt = v_ref[0, :, gl(gi)]
            outs = []
            for hh in range(PAIR):
                head = (step_id * ATTN_GROUPS + gi) * PAIR + hh
                s = s_all[hh * tail:(hh + 1) * tail] - ct_ref[0, head, :, 0:length]
                s = jnp.where(visible, s, NEG_INF)
                p = jnp.exp2(s - jnp.max(s, axis=-1, keepdims=True)).astype(BF16)
                acc = _dot(p, jnp.where(lane_head == hh, vt, one))
                outs.append(acc / pltpu.roll(acc, HEAD_DIM, axis=1))
            o_ref[0, rows, gl(gi)] = jnp.where(lane_head == 0, outs[0], outs[1]).astype(o_ref.dtype)


def _fox_prompt(qb, kb, vb, c_t):
    b, length, _ = qb.shape
    n_full, tail = divmod(length, QB)
    assert tail % 16 == 0 and c_t.shape[3] == (n_full + (1 if tail else 0)) * QB
    qkv = pl.BlockSpec((1, length, LANES * ATTN_GROUPS), lambda i, g: (i, 0, g))
    return pl.pallas_call(
        functools.partial(_attn_kernel, n_full=n_full, tail=tail),
        out_shape=jax.ShapeDtypeStruct(qb.shape, BF16),
        grid=(b, HEADS // (PAIR * ATTN_GROUPS)),
        in_specs=[qkv, qkv, qkv, pl.BlockSpec((1, HEADS, 1, c_t.shape[3]), lambda i, g: (i, 0, 0, 0))],
        out_specs=qkv,
        compiler_params=_cparams(("parallel", "arbitrary")),
        name="fox_prompt",
    )(qb, kb, vb, c_t)


def _head_sum_matrix():
    r = lax.broadcasted_iota(jnp.int32, (WIDTH, WIDTH), 0) // HEAD_DIM
    c = lax.broadcasted_iota(jnp.int32, (WIDTH, WIDTH), 1) // HEAD_DIM
    return (r == c).astype(BF16)


def _rwkv_pointwise(zs, p):
    (w0, w2, a0, a2, g2, k_k, k_a, r_k) = p
    r = zs[:, 0:WIDTH]
    k = zs[:, WIDTH:2 * WIDTH]
    v = zs[:, 2 * WIDTH:3 * WIDTH]
    zw = zs[:, ZW_OFF:ZW_OFF + LANES]
    za = zs[:, ZA_OFF:ZA_OFF + LANES]
    zg = zs[:, ZG_OFF:ZG_OFF + ZG_COLS]
    w_log = _log_sigmoid(w0 + _dot(jnp.tanh(zw).astype(BF16), w2)) - 0.5
    logdec = -jnp.exp(w_log)
    a = _sigmoid(a0 + _dot(za.astype(BF16), a2))
    g = _dot(_sigmoid(zg).astype(BF16), g2)
    hs = _head_sum_matrix()
    kk = k * k_k
    norm = jnp.sqrt(_split_dot(hs, kk * kk, 2, left=False))
    kk = kk / jnp.maximum(norm, 1e-12)
    km = k * (1.0 + (a - 1.0) * k_a)
    bonus = _split_dot(hs, r * km * r_k, 2, left=False) * v
    return r, km, v, kk, a, logdec, g, bonus


def _group_norm_out(y, ln_w, ln_b, bonus, g):
    mean = jnp.mean(y, axis=-1, keepdims=True)
    var = jnp.mean(jnp.square(y - mean), axis=-1, keepdims=True)
    yn = (y - mean) * lax.rsqrt(var + GN_EPS) * ln_w + ln_b
    return (yn + bonus) * g


RCHUNK = 128


def _rwkv_chunk_kernel(zr_ref, zp_ref, sh_ref, mu_ref, w0_ref, w2_ref, a0_ref, a2_ref, g2_ref, kk_ref, ka_ref,
                       rk_ref, lnw_ref, lnb_ref, out_ref, wkv_ref, s_ref, *, length):
    ci = pl.program_id(1)
    n = RCHUNK

    @pl.when(ci == 0)
    def _():
        s_ref[...] = jnp.zeros_like(s_ref)

    zr = zr_ref[0]
    row = lax.broadcasted_iota(jnp.int32, (n, 1), 0)
    prev = jnp.where(ci == 0, sh_ref[0], zp_ref[0, 7:8, :])
    zprev = jnp.where(row == 0, prev, pltpu.roll(zr, 1, axis=0))
    valid = (ci * n + row) < length
    zs = jnp.where(valid, zr + (zprev - zr) * mu_ref[...], 0.0)
    params = (w0_ref[...], w2_ref[...], a0_ref[...], a2_ref[...], g2_ref[...], kk_ref[...], ka_ref[...],
              rk_ref[...])
    r, km, v, kk, a, logdec, g, bonus = _rwkv_pointwise(zs, params)
    logdec = jnp.where(valid, logdec, 0.0)

    ri = lax.broadcasted_iota(jnp.int32, (n, n), 0)
    cj = lax.broadcasted_iota(jnp.int32, (n, n), 1)
    incl = cj <= ri
    strict = cj < ri
    eye = (ri == cj).astype(F32)
    eye_h = eye[:HEAD_DIM, :HEAD_DIM]
    logp = _split_dot(incl.astype(BF16), logdec, 3)
    logp_c = logp[n - 1:n, :]
    e_pos = jnp.exp(logp)
    e_neg = jnp.exp(-logp)
    e_end = jnp.exp(logp_c - logp)
    rt_all = r * e_pos
    pc_all = jnp.exp(logp_c)
    heads = range(HEADS)
    sls = [slice(h * HEAD_DIM, (h + 1) * HEAD_DIM) for h in heads]
    cut = lambda x: [x[:, sl].astype(BF16) for sl in sls]
    rt, kt, at, bt = cut(rt_all), cut(km * e_neg), cut(-kk * jnp.exp(logp - logdec)), cut(kk * a * e_neg)
    bh, kh, vh = cut(kk * a * e_end), cut(km * e_end), cut(v)

    m_all = [_dot_nt(jnp.concatenate([at[h], rt[h]], axis=0), jnp.concatenate([bt[h], kt[h]], axis=0)) for h in heads]
    a_ab = [jnp.where(strict, m[:n, :n], 0.0) for m in m_all]
    a_ak = [jnp.where(strict, m[:n, n:], 0.0).astype(BF16) for m in m_all]
    m_rb = [jnp.where(incl, m[n:, :n], 0.0).astype(BF16) for m in m_all]
    m_rk = [jnp.where(incl, m[n:, n:], 0.0).astype(BF16) for m in m_all]
    t_inv = [eye for _ in heads]
    pw = a_ab
    span = 1
    while span < n:
        pt = [_dot(pw[h].astype(BF16), jnp.concatenate([pw[h], t_inv[h]], axis=1).astype(BF16)) for h in heads]
        pw = [x[:, :n] for x in pt]
        t_inv = [t_inv[h] + pt[h][:, n:] for h in heads]
        span *= 2
    t_bf = [x.astype(BF16) for x in t_inv]
    w_m = [_dot(t_bf[h], at[h]) for h in heads]
    akv = [_dot(a_ak[h], vh[h]).astype(BF16) for h in heads]
    u0 = [_dot(t_bf[h], akv[h]) for h in heads]
    w_bf = [x.astype(BF16) for x in w_m]
    u_bf = [x.astype(BF16) for x in u0]
    qp = [rt_all[:, sls[h]] + _dot(m_rb[h], w_bf[h]) for h in heads]
    y0 = [_dot(m_rb[h], u_bf[h]) + _dot(m_rk[h], vh[h]) for h in heads]
    gt = [eye_h * pc_all[:, sls[h]] + _dot_tn(w_m[h], bh[h]) for h in heads]
    et = [_dot_tn(u0[h], bh[h]) + _dot_tn(v[:, sls[h]], kh[h]) for h in heads]
    for h in heads:
        sl = sls[h]
        s_h = s_ref[h].astype(BF16)
        y = _dot_nt(qp[h].astype(BF16), s_h) + y0[h]
        s_ref[h] = _dot(s_h, gt[h].astype(BF16)) + et[h]
        out_ref[0, :, sl] = _group_norm_out(y, lnw_ref[:, sl], lnb_ref[:, sl], bonus[:, sl],
                                            g[:, sl]).astype(out_ref.dtype)

    @pl.when(ci == pl.num_programs(1) - 1)
    def _():
        wkv_ref[0] = s_ref[...]


def _rwkv_prompt(zr, shift0, rp):
    b, length, _ = zr.shape
    n = RCHUNK
    nchunk = pl.cdiv(length, n)
    vec = _full((1, WIDTH))
    return pl.pallas_call(
        functools.partial(_rwkv_chunk_kernel, length=length),
        out_shape=[jax.ShapeDtypeStruct((b, length, WIDTH), BF16),
                   jax.ShapeDtypeStruct((b, HEADS, HEAD_DIM, HEAD_DIM), F32)],
        grid=(b, nchunk),
        in_specs=[pl.BlockSpec((1, n, R_COLS), lambda i, c: (i, c, 0)),
                  pl.BlockSpec((1, 8, R_COLS), lambda i, c: (i, jnp.maximum(c * (n // 8) - 1, 0), 0)),
                  pl.BlockSpec((1, 1, R_COLS), lambda i, c: (i, 0, 0)),
                  _full((1, R_COLS)), vec, _full((LANES, WIDTH)), vec, _full((LANES, WIDTH)),
                  _full((ZG_COLS, WIDTH)), vec, vec, vec, vec, vec],
        out_specs=[pl.BlockSpec((1, n, WIDTH), lambda i, c: (i, c, 0)),
                   pl.BlockSpec((1, HEADS, HEAD_DIM, HEAD_DIM), lambda i, c: (i, 0, 0, 0))],
        scratch_shapes=[pltpu.VMEM((HEADS, HEAD_DIM, HEAD_DIM), F32)],
        compiler_params=_cparams(("parallel", "arbitrary")),
        name="rwkv_prompt",
    )(zr, zr, shift0, rp["mu"], rp["w0"], rp["w2"], rp["a0"], rp["a2"], rp["g2"], rp["k_k"], rp["k_a"],
      rp["r_k"], rp["ln_w"], rp["ln_b"])


DEC_ROWS = 8


def _rwkv_decode_kernel(zr_ref, sh_ref, st_ref, mu_ref, w0_ref, w2_ref, a0_ref, a2_ref, g2_ref, kk_ref, ka_ref,
                        rk_ref, lnw_ref, lnb_ref, out_ref, ns_ref, vec_ref):
    zr = zr_ref[...]
    zs = zr + (sh_ref[...] - zr) * mu_ref[...]
    params = (w0_ref[...], w2_ref[...], a0_ref[...], a2_ref[...], g2_ref[...], kk_ref[...], ka_ref[...],
              rk_ref[...])
    r, km, v, kk, a, logdec, g, bonus = _rwkv_pointwise(zs, params)
    for idx, x in enumerate((r, km, v, kk, kk * a, jnp.exp(logdec), g, bonus)):
        for s in range(DEC_ROWS):
            vec_ref[idx, s] = x[s:s + 1, :]
    ri = lax.broadcasted_iota(jnp.int32, (HEAD_DIM, HEAD_DIM), 0)
    cj = lax.broadcasted_iota(jnp.int32, (HEAD_DIM, HEAD_DIM), 1)
    eye = ri == cj

    def per_row(s, _):
        for h in range(HEADS):
            sl = slice(h * HEAD_DIM, (h + 1) * HEAD_DIM)
            r_h, k_h, v_h, kk_h, ka_h, w_h, g_h, bo_h = (vec_ref[i, s, :, sl] for i in range(8))
            st = st_ref[s, h]
            sa = jnp.sum(st * (-kk_h), axis=-1, keepdims=True)
            v_col = jnp.sum(jnp.where(eye, v_h, 0.0), axis=-1, keepdims=True)
            st = st * w_h + sa * ka_h + v_col * k_h
            ns_ref[s, h] = st
            y_col = jnp.sum(st * r_h, axis=-1, keepdims=True)
            y = jnp.sum(jnp.where(eye, y_col, 0.0), axis=0, keepdims=True)
            out_ref[s, :, sl] = _group_norm_out(y, lnw_ref[:, sl], lnb_ref[:, sl], bo_h, g_h)
        return 0

    lax.fori_loop(0, DEC_ROWS, per_row, 0)


def _rwkv_decode(zr, shift, state, rp):
    n = zr.shape[0]
    vec = _full((1, WIDTH))
    rows = lambda c: pl.BlockSpec((DEC_ROWS, c), lambda i: (i, 0))
    st = pl.BlockSpec((DEC_ROWS, HEADS, HEAD_DIM, HEAD_DIM), lambda i: (i, 0, 0, 0))
    return pl.pallas_call(
        _rwkv_decode_kernel,
        out_shape=[jax.ShapeDtypeStruct((n, 1, WIDTH), F32), jax.ShapeDtypeStruct(state.shape, F32)],
        grid=(n // DEC_ROWS,),
        in_specs=[rows(R_COLS), rows(R_COLS), st, _full((1, R_COLS)), vec, _full((LANES, WIDTH)), vec,
                  _full((LANES, WIDTH)), _full((ZG_COLS, WIDTH)), vec, vec, vec, vec, vec],
        out_specs=[pl.BlockSpec((DEC_ROWS, 1, WIDTH), lambda i: (i, 0, 0)), st],
        scratch_shapes=[pltpu.VMEM((8, DEC_ROWS, 1, WIDTH), F32)],
        compiler_params=_cparams(("parallel",)),
        name="rwkv_decode",
    )(zr, shift, state, rp["mu"], rp["w0"], rp["w2"], rp["a0"], rp["a2"], rp["g2"], rp["k_k"], rp["k_a"],
      rp["r_k"], rp["ln_w"], rp["ln_b"])


SUFFIX_PAGES = 512
DEC_PAGES = 32


def _page_suffix_kernel(lf_ref, et_ref):
    pages, heads, plen = lf_ref.shape
    x = lf_ref[...].reshape(pages * heads, plen)
    u = lax.broadcasted_iota(jnp.int32, (plen, plen), 0)
    t = lax.broadcasted_iota(jnp.int32, (plen, plen), 1)
    after = _split_dot((u > t).astype(BF16), x, 3, left=False)
    total = _split_dot(jnp.ones((plen, plen), BF16), x, 3, left=False)
    et_ref[:, 0] = (after * LOG2E).reshape(pages, heads, plen)
    et_ref[:, 1] = (total * LOG2E).reshape(pages, heads, plen)


def _page_suffix(lft):
    n_pool, heads, plen = lft.shape
    return pl.pallas_call(
        _page_suffix_kernel,
        out_shape=jax.ShapeDtypeStruct((n_pool, 2, heads, plen), F32),
        grid=(pl.cdiv(n_pool, SUFFIX_PAGES),),
        in_specs=[pl.BlockSpec((SUFFIX_PAGES, heads, plen), lambda i: (i, 0, 0))],
        out_specs=pl.BlockSpec((SUFFIX_PAGES, 2, heads, plen), lambda i: (i, 0, 0, 0)),
        compiler_params=_cparams(("parallel",)),
        name="page_suffix",
    )(lft)


def _fox_decode_kernel(pt_ref, q_ref, kn_ref, vn_ref, lfn_ref, *refs):
    del pt_ref
    k_refs, v_refs, et_refs = (refs[i * DEC_PAGES:(i + 1) * DEC_PAGES] for i in range(3))
    o_ref, m_ref, l_ref, acc_ref, carry_ref = refs[3 * DEC_PAGES:]
    g = pl.program_id(1)
    plen = k_refs[0].shape[4]

    @pl.when(g == 0)
    def _():
        m_ref[...] = jnp.full_like(m_ref, NEG_INF)
        l_ref[...] = jnp.zeros_like(l_ref)
        acc_ref[...] = jnp.zeros_like(acc_ref)
        carry_ref[...] = lfn_ref[0] * LOG2E

    for h in range(HEADS):
        row = slice(h, h + 1)
        q_b = jnp.broadcast_to(q_ref[0, h], (HEAD_DIM, plen))
        carry = carry_ref[row, :]
        scores = []
        for i in range(DEC_PAGES):
            s = jnp.sum(k_refs[i][0, 0, h] * q_b, axis=0, keepdims=True)
            scores.append(s + et_refs[i][0, 0, row, :] + carry)
            carry = carry + et_refs[i][0, 1, row, 0:1]
        carry_ref[row, :] = carry
        top = scores[0]
        for s in scores[1:]:
            top = jnp.maximum(top, s)
        m_old = m_ref[row, :]
        m_new = jnp.maximum(m_old, jnp.max(top, axis=-1, keepdims=True))
        alpha = jnp.exp2(m_old - m_new)
        m_ref[row, :] = m_new
        psum = None
        pv = None
        for i in range(DEC_PAGES):
            pr = jnp.exp2(scores[i] - m_new)
            part = v_refs[i][0, 0, h] * pr
            psum = pr if psum is None else psum + pr
            pv = part if pv is None else pv + part
        l_ref[row, :] = alpha * l_ref[row, :] + jnp.sum(psum, axis=-1, keepdims=True)
        acc_ref[h] = alpha * acc_ref[h] + pv

    @pl.when(g == pl.num_programs(1) - 1)
    def _():
        for h in range(HEADS):
            row = slice(h, h + 1)
            s_new = jnp.sum(q_ref[0, h] * kn_ref[0, h], axis=0, keepdims=True)
            m_fin = jnp.maximum(m_ref[row, :], s_new)
            a_fin = jnp.exp2(m_ref[row, :] - m_fin)
            p_new = jnp.exp2(s_new - m_fin)
            num = a_fin * jnp.sum(acc_ref[h], axis=-1, keepdims=True) + p_new * vn_ref[0, h]
            o_ref[0, h] = num / (a_fin * l_ref[row, :] + p_new)


def _fox_decode(page_table, q, k_new, v_new, lf_new, cache_kt, cache_vt, et):
    n, n_pages = page_table.shape
    page = cache_kt.shape[4]
    assert n_pages % DEC_PAGES == 0
    tok = pl.BlockSpec((1, HEADS, HEAD_DIM, 1), lambda s, g, pt: (s, 0, 0, 0))

    def paged(shape, i, lead):
        idx = lambda s, g, pt: lead + (pt[s, n_pages - 1 - (g * DEC_PAGES + i)],) + (0,) * (len(shape) - len(lead) - 1)
        return pl.BlockSpec(shape, idx)

    slots = range(DEC_PAGES)
    kv_shape = (1, 1, HEADS, HEAD_DIM, page)
    grid_spec = pltpu.PrefetchScalarGridSpec(
        num_scalar_prefetch=1,
        grid=(n, n_pages // DEC_PAGES),
        in_specs=[tok, tok, tok, pl.BlockSpec((1, HEADS, 1), lambda s, g, pt: (s, 0, 0))]
        + [paged(kv_shape, i, (0,)) for i in slots] + [paged(kv_shape, i, (0,)) for i in slots]
        + [paged((1, 2, HEADS, page), i, ()) for i in slots],
        out_specs=tok,
        scratch_shapes=[pltpu.VMEM((HEADS, 1), F32), pltpu.VMEM((HEADS, 1), F32),
                        pltpu.VMEM((HEADS, HEAD_DIM, page), F32), pltpu.VMEM((HEADS, 1), F32)],
    )
    return pl.pallas_call(
        _fox_decode_kernel,
        out_shape=jax.ShapeDtypeStruct((n, HEADS, HEAD_DIM, 1), F32),
        grid_spec=grid_spec,
        compiler_params=_cparams(("parallel", "arbitrary")),
        name="fox_decode",
    )(page_table, q, k_new, v_new, lf_new, *([cache_kt] * DEC_PAGES), *([cache_vt] * DEC_PAGES), *([et] * DEC_PAGES))


def _outproj_kernel(oa_ref, orw_ref, sg_ref, x_ref, woa_ref, wor_ref, wout_ref, nf_ref, wrt_ref, brc_ref, *refs,
                    n_valid):
    x1_ref, h2t_ref, gate_ref, gate3_ref = refs[-4:]
    i = pl.program_id(0)
    d = x_ref.shape[1]
    n_exp = wrt_ref.shape[0]

    @pl.when(i < n_valid)
    def _():
        a = _dot(oa_ref[...], woa_ref[...])
        b = _dot(orw_ref[...], wor_ref[...])
        merged = sg_ref[:, :d].astype(F32) * a + sg_ref[:, d:].astype(F32) * b
        x1 = x_ref[...] + _dot(merged.astype(BF16), wout_ref[...])
        x1_ref[...] = x1
        h2 = _rms(x1, nf_ref[...])
        h2t_ref[...] = h2.T.astype(BF16)
        work = _dot_nt(wrt_ref[...], h2, HIGHEST) + brc_ref[...]
        row = lax.broadcasted_iota(jnp.int32, work.shape, 0)
        top = None
        denom = jnp.zeros((1, work.shape[1]), F32)
        picks = []
        for _ in range(TOP_K):
            mx = jnp.max(work, axis=0, keepdims=True)
            idx = jnp.min(jnp.where(work == mx, row, n_exp), axis=0, keepdims=True)
            hit = row == idx
            top = mx if top is None else top
            e = jnp.exp(mx - top)
            denom = denom + e
            picks.append((hit, e))
            work = jnp.where(hit, -jnp.inf, work)
        gate = jnp.zeros(work.shape, F32)
        for hit, e in picks:
            gate = jnp.where(hit, e / denom, gate)
        gate_ref[...] = gate
        for ex in range(n_exp):
            gate3_ref[ex] = gate[ex:ex + 1, :]

    @pl.when(i >= n_valid)
    def _():
        x1_ref[...] = jnp.zeros_like(x1_ref)
        h2t_ref[...] = jnp.zeros_like(h2t_ref)
        gate_ref[...] = jnp.zeros_like(gate_ref)
        gate3_ref[...] = jnp.zeros_like(gate3_ref)


def _outproj(o_attn, o_rwkv, sg, x, w_oa, w_or, w_out, norm_ffn, w_router_t, b_router_col, tm, m_pad, into=None,
             base=0):
    m, d = x.shape
    n_exp = w_router_t.shape[0]
    n_valid = m // tm
    assert m % tm == 0 and m_pad % tm == 0 and base % tm == 0
    last = n_valid - 1
    off = base // tm
    row = lambda n: pl.BlockSpec((tm, n), lambda i: (jnp.minimum(i, last), 0))
    operands = [o_attn, o_rwkv, sg, x, w_oa, w_or, w_out, norm_ffn, w_router_t, b_router_col]
    in_specs = [row(WIDTH), row(WIDTH), row(2 * d), row(d), _full((WIDTH, d)), _full((WIDTH, d)), _full((d, d)),
                _full((1, d)), _full((n_exp, d)), _full((n_exp, 1))]
    if into is None:
        out_shape = [jax.ShapeDtypeStruct((m_pad, d), F32), jax.ShapeDtypeStruct((d, m_pad), BF16),
                     jax.ShapeDtypeStruct((n_exp, m_pad), F32), jax.ShapeDtypeStruct((n_exp, 1, m_pad), F32)]
        aliases = {}
        steps = m_pad // tm
    else:
        out_shape = [jax.ShapeDtypeStruct(a.shape, a.dtype) for a in into]
        aliases = {len(operands) + k: k for k in range(len(into))}
        operands = operands + list(into)
        in_specs = in_specs + [pl.BlockSpec(memory_space=pl.ANY)] * len(into)
        steps = n_valid
    return pl.pallas_call(
        functools.partial(_outproj_kernel, n_valid=n_valid),
        out_shape=out_shape,
        grid=(steps,),
        in_specs=in_specs,
        out_specs=[pl.BlockSpec((tm, d), lambda i: (i + off, 0)), pl.BlockSpec((d, tm), lambda i: (0, i + off)),
                   pl.BlockSpec((n_exp, tm), lambda i: (0, i + off)),
                   pl.BlockSpec((n_exp, 1, tm), lambda i: (0, 0, i + off))],
        input_output_aliases=aliases,
        compiler_params=_cparams(("parallel",)),
        name="outproj_router",
    )(*operands)


MOE_TOKENS = 1792
MOE_SLOTS = 256
MOE_FF_CHUNK = 512


def _moe_kernel(h2t_ref, gate_ref, gate3_ref, wu_ref, bu_ref, wd_ref, bd_ref, o_ref, acc_ref, rank_ref, xg_ref,
                sg_ref, dn_ref):
    e = pl.program_id(1)
    last = pl.num_programs(1) - 1
    d, tt = h2t_ref.shape
    n_exp = gate_ref.shape[0]
    d_ff = wd_ref.shape[2]
    reps = MOE_SLOTS // LANES
    cols = lambda ref, lo, n: jnp.concatenate([ref[0, lo:lo + n, :]] * reps, axis=1)

    def gather(ex, blk):
        g_row = gate3_ref[ex]
        slot_of = jnp.where(g_row > 0.0, rank_ref[ex], -1.0)
        slot = (blk * MOE_SLOTS + lax.broadcasted_iota(jnp.int32, (MOE_SLOTS, 1), 0)).astype(F32)
        sel = slot_of == slot
        sel_one = jnp.where(sel, 1.0, 0.0).astype(BF16)
        return _dot_nt(h2t_ref[...], sel_one).astype(BF16), jnp.where(sel, g_row, 0.0).astype(BF16)

    def ffn(xg):
        def up(lo):
            return (_dot(wu_ref[0, lo:lo + MOE_FF_CHUNK, :], xg),
                    _dot(wu_ref[0, d_ff + lo:d_ff + lo + MOE_FF_CHUNK, :], xg))

        dn = None
        starts = list(range(0, d_ff, MOE_FF_CHUNK))
        ahead = up(starts[0])
        for ci, lo in enumerate(starts):
            x_glu, x_lin = ahead
            if ci + 1 < len(starts):
                ahead = up(starts[ci + 1])
            x_glu = jnp.minimum(x_glu + cols(bu_ref, lo, MOE_FF_CHUNK), SWIGLU_LIMIT)
            x_lin = jnp.clip(x_lin + cols(bu_ref, d_ff + lo, MOE_FF_CHUNK), -SWIGLU_LIMIT, SWIGLU_LIMIT)
            act = (x_glu * _sigmoid(SWIGLU_ALPHA * x_glu) * (x_lin + 1.0)).astype(BF16)
            part = _dot(wd_ref[0, :, lo:lo + MOE_FF_CHUNK], act)
            dn = part if dn is None else dn + part
        return jnp.concatenate([(dn[lo:lo + MOE_FF_CHUNK] + cols(bd_ref, lo, MOE_FF_CHUNK)).astype(BF16)
                                for lo in range(0, d, MOE_FF_CHUNK)], axis=0)

    def scatter(dn, sel_gate):
        for lo in range(0, d, MOE_FF_CHUNK):
            rows = slice(lo, lo + MOE_FF_CHUNK)
            acc_ref[rows, :] += _dot(dn[rows], sel_gate)

    @pl.when(e == 0)
    def _():
        acc_ref[...] = jnp.zeros_like(acc_ref)
        tr = lax.broadcasted_iota(jnp.int32, (tt, tt), 0)
        tc = lax.broadcasted_iota(jnp.int32, (tt, tt), 1)
        hit_all = jnp.where(gate_ref[...] > 0.0, 1.0, 0.0).astype(BF16)
        rank = _dot(hit_all, (tr < tc).astype(BF16))
        for ex in range(n_exp):
            rank_ref[ex] = rank[ex:ex + 1, :]
        xg0, sg0 = gather(0, 0)
        xg_ref[0] = xg0
        sg_ref[0] = sg0
        dn_ref[...] = jnp.zeros_like(dn_ref)
        sg_ref[2] = jnp.zeros_like(sg0)

    scatter(dn_ref[...], sg_ref[(e + 2) % 3])
    dn_new = ffn(xg_ref[e % 2])
    xg_next, sg_next = gather(jnp.minimum(e + 1, last), 0)
    xg_ref[(e + 1) % 2] = xg_next
    sg_ref[(e + 1) % 3] = sg_next
    dn_ref[...] = dn_new

    def extra_pass(blk, _):
        xg, sel_gate = gather(e, blk)
        scatter(ffn(xg), sel_gate)
        return 0

    count = jnp.sum(jnp.where(gate3_ref[e] > 0.0, 1.0, 0.0)).astype(jnp.int32)
    lax.fori_loop(1, (count + MOE_SLOTS - 1) // MOE_SLOTS, extra_pass, 0)

    @pl.when(e == last)
    def _():
        scatter(dn_ref[...], sg_ref[last % 3])
        o_ref[...] = acc_ref[...].T.astype(o_ref.dtype)


def _moe(h2t, gate, gate3, w_up_t, b_up, w_down_t, b_down, tt):
    d, m_pad = h2t.shape
    n_exp, d_up, _ = w_up_t.shape
    assert m_pad % tt == 0
    return pl.pallas_call(
        _moe_kernel,
        out_shape=jax.ShapeDtypeStruct((m_pad, d), BF16),
        grid=(m_pad // tt, n_exp),
        in_specs=[pl.BlockSpec((d, tt), lambda i, e: (0, i)),
                  pl.BlockSpec((n_exp, tt), lambda i, e: (0, i)),
                  pl.BlockSpec((n_exp, 1, tt), lambda i, e: (0, 0, i)),
                  pl.BlockSpec((1, d_up, d), lambda i, e: (e, 0, 0)),
                  pl.BlockSpec((1, d_up, LANES), lambda i, e: (e, 0, 0)),
                  pl.BlockSpec((1, d, w_down_t.shape[2]), lambda i, e: (e, 0, 0)),
                  pl.BlockSpec((1, d, LANES), lambda i, e: (e, 0, 0))],
        out_specs=pl.BlockSpec((tt, d), lambda i, e: (i, 0)),
        scratch_shapes=[pltpu.VMEM((d, tt), F32), pltpu.VMEM((n_exp, 1, tt), F32),
                        pltpu.VMEM((2, d, MOE_SLOTS), BF16), pltpu.VMEM((3, MOE_SLOTS, tt), BF16),
                        pltpu.VMEM((d, MOE_SLOTS), BF16)],
        compiler_params=_cparams(("parallel", "arbitrary")),
        name="moe_routed",
    )(h2t, gate, gate3, w_up_t, b_up, w_down_t, b_down)


def _final_kernel(x1_ref, moe_ref, g_ref, y_ref):
    y_ref[...] = _rms(x1_ref[...] + moe_ref[...].astype(F32), g_ref[...]).reshape(y_ref.shape)


def _final_norm_prompt(x1, moe, g, b, length, skip, tm):
    d = x1.shape[1]
    seq = length - skip
    assert seq % tm == 0 and skip % 8 == 0
    assert length % 8 == 0
    rows = pl.BlockSpec((pl.Element(tm), pl.Element(d)),
                        lambda i, j: (pl.multiple_of(i * length + skip + j * tm, 8), 0))
    return pl.pallas_call(
        _final_kernel,
        out_shape=jax.ShapeDtypeStruct((b, seq, d), F32),
        grid=(b, seq // tm),
        in_specs=[rows, rows, _full((1, d))],
        out_specs=pl.BlockSpec((1, tm, d), lambda i, j: (i, j, 0)),
        compiler_params=_cparams(("parallel", "parallel")),
        name="final_norm_prompt",
    )(x1, moe, g)


def _final_norm(x1, moe, g, m, tm, base):
    d = x1.shape[1]
    assert base % tm == 0 and m % tm == 0
    off = base // tm
    row = pl.BlockSpec((tm, d), lambda i: (i + off, 0))
    return pl.pallas_call(
        _final_kernel,
        out_shape=jax.ShapeDtypeStruct((m, d), F32),
        grid=(m // tm,),
        in_specs=[row, row, _full((1, d))],
        out_specs=pl.BlockSpec((tm, d), lambda i: (i, 0)),
        compiler_params=_cparams(("parallel",)),
        name="final_norm",
    )(x1, moe, g)


def _pad_cols(a, n):
    return jnp.pad(a, ((0, 0), (0, n - a.shape[1])))


def _pack_rwkv_cols(a):
    o = 3 * WIDTH
    return jnp.concatenate([
        a[:, :o], _pad_cols(a[:, o:o + DECAY_LORA], LANES),
        _pad_cols(a[:, o + DECAY_LORA:o + DECAY_LORA + AAA_LORA], LANES),
        _pad_cols(a[:, o + DECAY_LORA + AAA_LORA:], ZG_COLS)], axis=1)


def _unpack_rwkv_cols(a):
    return jnp.concatenate([a[..., :ZW_OFF], a[..., ZW_OFF:ZW_OFF + DECAY_LORA], a[..., ZA_OFF:ZA_OFF + AAA_LORA],
                            a[..., ZG_OFF:ZG_OFF + GATE_LORA]], axis=-1)


def _pad_rows(a, n):
    return jnp.pad(a, ((0, n - a.shape[0]), (0, 0)))


def kernel(x_prompt, x_sample, cache_k, cache_v, cache_logf, page_table, state_wkv, state_shift, meta_tokens, norm_mix, w_in, b_f, w_oa, rwkv_mu, rwkv_w0, rwkv_w2, rwkv_a0, rwkv_a2, rwkv_g2, rwkv_k_k, rwkv_k_a, rwkv_r_k, rwkv_ln_w, rwkv_ln_b, w_or, w_out, norm_ffn, w_router, b_router, w_up, b_up, w_down, b_down, norm_final):
    b, seq, d = x_prompt.shape
    db = x_sample.shape[0]
    depth = w_in.shape[0]
    assert depth == 1 and x_sample.shape[1] == 1
    length = N_META + seq
    n_exp = w_router.shape[2]
    a_cols = 3 * WIDTH + HEADS
    r_cols = 3 * WIDTH + DECAY_LORA + AAA_LORA + GATE_LORA

    wi = w_in[0]
    w_packed = jnp.concatenate([
        wi[:, :QKV_COLS], _pad_cols(wi[:, QKV_COLS:a_cols], F_COLS),
        _pack_rwkv_cols(wi[:, a_cols:a_cols + r_cols]), wi[:, a_cols + r_cols:]], axis=1).astype(BF16)
    bf_pad = _pad_cols(b_f[0][None, :], F_COLS)
    rp = {
        "mu": _pack_rwkv_cols(rwkv_mu[0][None, :]),
        "w0": rwkv_w0[0][None, :], "a0": rwkv_a0[0][None, :],
        "w2": _pad_rows(rwkv_w2[0], LANES).astype(BF16), "a2": _pad_rows(rwkv_a2[0], LANES).astype(BF16),
        "g2": _pad_rows(rwkv_g2[0], ZG_COLS).astype(BF16),
        "k_k": rwkv_k_k[0][None, :], "k_a": rwkv_k_a[0][None, :], "r_k": rwkv_r_k[0].reshape(1, WIDTH),
        "ln_w": rwkv_ln_w[0][None, :], "ln_b": rwkv_ln_b[0][None, :],
    }
    g_mix = norm_mix[0][None, :]
    g_ffn = norm_ffn[0][None, :]
    g_fin = norm_final[None, :]
    woa, wor, wout = w_oa[0].astype(BF16), w_or[0].astype(BF16), w_out[0].astype(BF16)
    wr_t = jnp.transpose(w_router[0])
    br_col = b_router[0][:, None]
    wu_t = jnp.transpose(w_up[0], (0, 2, 1)).astype(BF16)
    wd_t = jnp.transpose(w_down[0], (0, 2, 1)).astype(BF16)
    bu = jnp.broadcast_to(b_up[0][:, :, None], b_up.shape[1:] + (LANES,))
    bd = jnp.broadcast_to(b_down[0][:, :, None], b_down.shape[1:] + (LANES,))

    meta = jnp.broadcast_to(meta_tokens.astype(x_prompt.dtype)[None], (b, N_META, d))
    m_p = b * length
    xp = jnp.concatenate([meta, x_prompt], axis=1).reshape(m_p, d)
    qb, kb, vb, k_p, v_p, lf_p, zr_p, sg_p = _inproj(xp, g_mix, w_packed, bf_pad, 512)
    c2 = _cumsum_logf(lf_p.reshape(b, length, F_COLS))
    lpad = pl.cdiv(length, QB) * QB
    c_t = jnp.pad(jnp.transpose(c2[:, :, :HEADS], (0, 2, 1)), ((0, 0), (0, 0), (0, lpad - length)))[:, :, None, :]
    as3 = lambda a: a.reshape(b, length, a.shape[-1])
    o_attn = _fox_prompt(as3(qb), as3(kb), as3(vb), c_t)
    zr3 = as3(zr_p)
    o_rwkv, wkv_p = _rwkv_prompt(zr3, jnp.zeros((b, 1, R_COLS), F32), rp)
    m_pad = pl.cdiv(m_p + db, MOE_TOKENS) * MOE_TOKENS
    routed_p = _outproj(o_attn.reshape(m_p, WIDTH), o_rwkv.reshape(m_p, WIDTH), sg_p, xp, woa, wor, wout, g_ffn, wr_t,
                        br_col, 256, m_pad)

    xs = x_sample.reshape(db, d)
    qs, _, _, k_s, v_s, lf_s, zr_s, sg_s = _inproj(xs, g_mix, w_packed, bf_pad, db)
    col = lambda a: a.reshape(db, HEADS, HEAD_DIM, 1)
    to_lanes = lambda c: jnp.transpose(c, (0, 1, 3, 4, 2))
    et_page = _page_suffix(jnp.transpose(cache_logf[0], (0, 2, 1)))
    o_attn_s = _fox_decode(page_table, col(qs.astype(F32)), col(k_s), col(v_s), lf_s[:, :HEADS, None],
                           to_lanes(cache_k), to_lanes(cache_v), et_page)
    o_rwkv_s, wkv_s = _rwkv_decode(zr_s, _pack_rwkv_cols(state_shift[0, :, 0, :]), state_wkv[0], rp)
    x1, h2t, gate, gate3 = _outproj(o_attn_s.reshape(db, WIDTH).astype(BF16), o_rwkv_s.reshape(db, WIDTH).astype(BF16),
                                    sg_s, xs, woa, wor, wout, g_ffn, wr_t, br_col, db, m_pad, into=routed_p, base=m_p)

    moe = _moe(h2t, gate, gate3, wu_t, bu, wd_t, bd, MOE_TOKENS)
    y_prompt = _final_norm_prompt(x1, moe, g_fin, b, length, N_META, 512)
    y_s = _final_norm(x1, moe, g_fin, db, db, m_p)

    y_sample = y_s.reshape(db, 1, d)
    hd = lambda a, n, t: a.reshape(1, n, t, HEADS, HEAD_DIM)
    logf_prompt = lf_p[:, :HEADS].reshape(1, b, length, HEADS)
    logf_sample = lf_s[:, :HEADS].reshape(1, db, 1, HEADS)
    shift_prompt = _unpack_rwkv_cols(zr3[:, length - 1:length, :])[None]
    shift_sample = _unpack_rwkv_cols(zr_s)[None, :, None, :]
    return (y_prompt, y_sample, hd(k_p, b, length), hd(v_p, b, length), logf_prompt, hd(k_s, db, 1),
            hd(v_s, db, 1), logf_sample, wkv_p[None], shift_prompt, wkv_s[None], shift_sample)
```
